```python
import math
import jax, jax.numpy as jnp
from jax import lax
import numpy as np

D_MODEL = 1024
BATCH = 4
SEQ = 8192
DEPTH = 1

POOL_WIDTH = D_MODEL // 2
POOL_WINDOWS = (2, 4, 8, 16)
POOL_GROUPS = len(POOL_WINDOWS)
POOL_GC = POOL_WIDTH // POOL_GROUPS
HEAD_DIM = 64
ATTN_HEADS = (D_MODEL // 2) // HEAD_DIM
ATTN_WIDTH = ATTN_HEADS * HEAD_DIM
MOBA_BLOCK = 256
MOBA_TOPK = 3
QUERY_CHUNK = 64
ROPE_THETA = 500000.0
ROT_DIM = HEAD_DIM // 4
N_BRANCH = 2
IN_WIDTH = 2 * POOL_WIDTH + 4 * ATTN_WIDTH + N_BRANCH * D_MODEL
EPS = 1e-6

kernel_name = "hybrid_pool_moba_gated_adaln"


def rms_norm(x, g):
    xf = x.astype(jnp.float32)
    y = xf * lax.rsqrt(jnp.mean(xf * xf, axis=-1, keepdims=True) + EPS)
    return (y * g.astype(jnp.float32)).astype(x.dtype)


def partial_rope(x):
    S_ = x.shape[1]
    half = ROT_DIM // 2
    pos = jnp.arange(S_, dtype=jnp.float32)
    inv_freq = ROPE_THETA ** (-jnp.arange(0, ROT_DIM, 2, dtype=jnp.float32) / ROT_DIM)
    ang = pos[:, None] * inv_freq[None, :]
    cos = jnp.cos(ang)[None, :, None, :].astype(x.dtype)
    sin = jnp.sin(ang)[None, :, None, :].astype(x.dtype)
    x1 = x[..., :half]
    x2 = x[..., half:ROT_DIM]
    return jnp.concatenate([x1 * cos - x2 * sin, x2 * cos + x1 * sin, x[..., ROT_DIM:]], axis=-1)


def multiscale_pool(u, w_grp, scale):
    B_, S_, _ = u.shape
    ug = u.reshape(B_, S_, POOL_GROUPS, POOL_GC)
    uf = ug.astype(jnp.float32)
    cs = lax.cumsum(uf, axis=1)
    cs = jnp.concatenate([jnp.zeros_like(cs[:, :1]), cs], axis=1)
    t = jnp.arange(S_)
    means = []
    for g, w in enumerate(POOL_WINDOWS):
        hi = t + 1
        lo = jnp.maximum(t + 1 - w, 0)
        cnt = (hi - lo).astype(jnp.float32)[None, :, None]
        means.append((cs[:, hi, g] - cs[:, lo, g]) / cnt)
    d = (jnp.stack(means, axis=2) - uf).astype(u.dtype)
    y = jnp.einsum('bsgc,gcd->bsgd', d, w_grp).reshape(B_, S_, POOL_WIDTH)
    return y * scale


def moba_attention(q, k, v):
    B_, S_, H, Dh = q.shape
    q = q.transpose(0, 2, 1, 3)
    k = k.transpose(0, 2, 1, 3)
    v = v.transpose(0, 2, 1, 3)
    nb = -(-S_ // MOBA_BLOCK)
    s_pad = nb * MOBA_BLOCK
    pad = ((0, 0), (0, 0), (0, s_pad - S_), (0, 0))
    kb = jnp.pad(k, pad).reshape(B_, H, nb, MOBA_BLOCK, Dh)
    vb = jnp.pad(v, pad).reshape(B_, H, nb, MOBA_BLOCK, Dh)
    kmean = jnp.mean(kb.astype(jnp.float32), axis=3)
    n_sel = min(MOBA_TOPK, nb)
    nq = S_ // QUERY_CHUNK
    q_chunks = q.reshape(B_, H, nq, QUERY_CHUNK, Dh).transpose(2, 0, 1, 3, 4)
    scale = Dh ** -0.5
    bi = jnp.arange(B_)[:, None, None, None]
    hi = jnp.arange(H)[None, :, None, None]
    blk_ids = jnp.arange(nb)
    kr = jnp.arange(MOBA_BLOCK)

    def one_chunk(args):
        ci, qc = args
        q0 = ci * QUERY_CHUNK
        qblk = q0 // MOBA_BLOCK
        qpos = q0 + jnp.arange(QUERY_CHUNK)
        gate = jnp.einsum('bhqd,bhnd->bhqn', qc.astype(jnp.float32), kmean)
        gate = jnp.where(blk_ids < qblk, gate, -jnp.inf)
        _, sel = lax.top_k(gate, n_sel)
        sel_ok = sel < qblk
        ksel = kb[bi, hi, sel]
        vsel = vb[bi, hi, sel]
        s_sel = jnp.einsum('bhqd,bhqkjd->bhqkj', qc, ksel).astype(jnp.float32) * scale
        s_sel = jnp.where(sel_ok[..., None], s_sel, -jnp.inf)
        s_sel = s_sel.reshape(B_, H, QUERY_CHUNK, n_sel * MOBA_BLOCK)
        kown = lax.dynamic_index_in_dim(kb, qblk, axis=2, keepdims=False)
        vown = lax.dynamic_index_in_dim(vb, qblk, axis=2, keepdims=False)
        kpos = qblk * MOBA_BLOCK + kr
        s_own = jnp.einsum('bhqd,bhjd->bhqj', qc, kown).astype(jnp.float32) * scale
        s_own = jnp.where(kpos[None, :] <= qpos[:, None], s_own, -jnp.inf)
        p = jax.nn.softmax(jnp.concatenate([s_sel, s_own], axis=-1), axis=-1).astype(v.dtype)
        p_sel = p[..., :n_sel * MOBA_BLOCK].reshape(B_, H, QUERY_CHUNK, n_sel, MOBA_BLOCK)
        p_own = p[..., n_sel * MOBA_BLOCK:]
        return (jnp.einsum('bhqkj,bhqkjd->bhqd', p_sel, vsel)
                + jnp.einsum('bhqj,bhjd->bhqd', p_own, vown))

    out = lax.map(one_chunk, (jnp.arange(nq), q_chunks))
    return out.transpose(1, 0, 3, 2, 4).reshape(B_, S_, H * Dh)


def setup_inputs(seed: int = 0) -> dict:
    key = jax.random.key(seed)
    ks = jax.random.split(key, 14)
    D = D_MODEL
    nrm = jax.random.normal
    return {
        "x": nrm(ks[0], (BATCH, SEQ, D), jnp.float32),
        "c": nrm(ks[1], (BATCH, D), jnp.float32),
        "w_ada": nrm(ks[2], (DEPTH, D, 3 * D), jnp.float32) * (0.3 * D ** -0.5),
        "b_ada": nrm(ks[3], (DEPTH, 3 * D), jnp.float32) * 0.01,
        "g_norm": 1.0 + 0.05 * nrm(ks[4], (DEPTH, D), jnp.float32),
        "w_in": nrm(ks[5], (DEPTH, D, IN_WIDTH), jnp.float32) * D ** -0.5,
        "w_pool_grp": nrm(ks[6], (DEPTH, POOL_GROUPS, POOL_GC, POOL_GC), jnp.float32) * POOL_GC ** -0.5,
        "pool_scale": 1.0 + 0.1 * nrm(ks[7], (DEPTH, POOL_WIDTH), jnp.float32),
        "w_pool_up": nrm(ks[8], (DEPTH, POOL_WIDTH, D), jnp.float32) * POOL_WIDTH ** -0.5,
        "w_attn_up": nrm(ks[9], (DEPTH, ATTN_WIDTH, D), jnp.float32) * ATTN_WIDTH ** -0.5,
        "w_out": nrm(ks[10], (DEPTH, D, D), jnp.float32) * D ** -0.5,
        "g_final": 1.0 + 0.05 * nrm(ks[11], (D,), jnp.float32),
    }


def reference(x, c, w_ada, b_ada, g_norm, w_in, w_pool_grp, pool_scale, w_pool_up, w_attn_up, w_out, g_final):
    B_, S_, D = x.shape
    splits = np.cumsum([POOL_WIDTH, POOL_WIDTH, ATTN_WIDTH, ATTN_WIDTH, ATTN_WIDTH, ATTN_WIDTH, D_MODEL]).tolist()
    for l in range(DEPTH):
        mod = c @ w_ada[l] + b_ada[l]
        shift, scl, gate = jnp.split(mod, 3, axis=-1)
        h = rms_norm(x, g_norm[l]) * (1.0 + scl[:, None, :]) + shift[:, None, :]
        proj = h @ w_in[l]
        u_pool, z_pool, q, k, v, z_attn, m_pool, m_attn = jnp.split(proj, splits, axis=-1)
        y_pool = multiscale_pool(u_pool, w_pool_grp[l], pool_scale[l]) * jax.nn.silu(z_pool)
        y_pool = y_pool @ w_pool_up[l]
        q = partial_rope(q.reshape(B_, S_, ATTN_HEADS, HEAD_DIM))
        k = partial_rope(k.reshape(B_, S_, ATTN_HEADS, HEAD_DIM))
        v = v.reshape(B_, S_, ATTN_HEADS, HEAD_DIM)
        y_attn = moba_attention(q, k, v) * jax.nn.silu(z_attn)
        y_attn = y_attn @ w_attn_up[l]
        merged = jax.nn.sigmoid(m_pool) * y_pool + jax.nn.sigmoid(m_attn) * y_attn
        x = x + gate[:, None, :] * (merged @ w_out[l])
    return rms_norm(x, g_final)
```

```python
import functools

import jax
import jax.numpy as jnp
from jax import lax
from jax.experimental import pallas as pl
from jax.experimental.pallas import tpu as pltpu

D_MODEL = 1024
POOL_WIDTH = 512
POOL_WINDOWS = (2, 4, 8, 16)
POOL_GC = 128
POOL_HIST = 16
HEAD_DIM = 64
N_HEADS = 8
ATTN_WIDTH = N_HEADS * HEAD_DIM
MOBA_BLOCK = 256
MOBA_TOPK = 3
ROT_DIM = 16
ROT_HALF = ROT_DIM // 2
ROPE_THETA = 500000.0
EPS = 1e-6

TM_QKV = 512
TM_OUT = 256
VMEM_LIMIT_BYTES = 56 * 1024 * 1024

_BF16 = jnp.bfloat16
_F32 = jnp.float32
_NT_DIMS = (((1,), (1,)), ((), ()))


def _adaln_norm(x, g, mod_ref):
    ms = jnp.mean(x * x, axis=-1, keepdims=True)
    y = x * lax.rsqrt(ms + EPS) * g
    shift = mod_ref[0, 0:1, :]
    scl = mod_ref[0, 1:2, :]
    return y * (1.0 + scl) + shift


def _mod_kernel(c_ref, w_ref, b_ref, o_ref):
    o_ref[...] = jnp.dot(c_ref[...], w_ref[...], preferred_element_type=_F32,
                         precision=lax.Precision.HIGHEST) + b_ref[...]


def _adaln_mod(c, w_ada, b_ada):
    nb_, d = c.shape
    rows = -(-nb_ // 8) * 8
    c_pad = jnp.zeros((rows, d), _F32).at[:nb_].set(c)
    n_out = w_ada.shape[1]
    out = pl.pallas_call(
        _mod_kernel,
        grid=(n_out // d,),
        in_specs=[pl.BlockSpec((rows, d), lambda j: (0, 0)),
                  pl.BlockSpec((d, d), lambda j: (0, j)),
                  pl.BlockSpec((1, d), lambda j: (0, j))],
        out_specs=pl.BlockSpec((rows, d), lambda j: (0, j)),
        out_shape=jax.ShapeDtypeStruct((rows, n_out), _F32),
        name="adaln_mod",
    )(c_pad, w_ada, b_ada.reshape(1, n_out))
    return out[:nb_].reshape(nb_, n_out // d, d)


def _qkv_kernel(x_ref, mod_ref, g_ref, wqv_ref, wk_ref, cos_t_ref, sin_t_ref, cos_l_ref, sin_l_ref,
                qt_ref, k_ref, vt_ref, km_ref):
    tm = x_ref.shape[1]
    h = _adaln_norm(x_ref[0], g_ref[...], mod_ref).astype(_BF16)

    qv_t = lax.dot_general(wqv_ref[...], h, _NT_DIMS, preferred_element_type=_F32)
    cos_t = cos_t_ref[...]
    sin_t = sin_t_ref[...]
    pieces = []
    for hd in range(N_HEADS):
        base = hd * HEAD_DIM
        x1 = qv_t[base:base + ROT_HALF]
        x2 = qv_t[base + ROT_HALF:base + ROT_DIM]
        pieces += [x1 * cos_t - x2 * sin_t, x2 * cos_t + x1 * sin_t, qv_t[base + ROT_DIM:base + HEAD_DIM]]
    q_t = (jnp.concatenate(pieces, axis=0) * (HEAD_DIM ** -0.5)).astype(_BF16)
    v_t = qv_t[ATTN_WIDTH:].astype(_BF16)

    k = jnp.dot(h, wk_ref[...], preferred_element_type=_F32)
    cos_l = jnp.concatenate([cos_l_ref[...]] * (ATTN_WIDTH // 128), axis=1)
    sin_l = jnp.concatenate([sin_l_ref[...]] * (ATTN_WIDTH // 128), axis=1)
    dim = lax.broadcasted_iota(jnp.int32, k.shape, 1) % HEAD_DIM
    partner = jnp.where(dim < ROT_HALF,
                        pltpu.roll(k, ATTN_WIDTH - ROT_HALF, axis=1),
                        pltpu.roll(k, ROT_HALF, axis=1))
    k = jnp.where(dim < ROT_DIM, k * cos_l + partner * sin_l, k)

    for i in range(tm // MOBA_BLOCK):
        rows = slice(i * MOBA_BLOCK, (i + 1) * MOBA_BLOCK)
        kb = k[rows]
        km_ref[0, i] = jnp.mean(kb, axis=0, keepdims=True)
        k_ref[0, i] = kb.astype(_BF16)
        qt_ref[0, i] = q_t[:, rows]
        vt_ref[0, i] = v_t[:, rows]


def _qkv_proj(x, mod, g_norm, wqv_t, wk, cos_t, sin_t, cos_l, sin_l):
    b, s, d = x.shape
    nb = s // MOBA_BLOCK
    tm = TM_QKV
    nbt = tm // MOBA_BLOCK
    full = lambda *shape: pl.BlockSpec(shape, lambda bi, ti: (0,) * len(shape))
    return pl.pallas_call(
        _qkv_kernel,
        grid=(b, s // tm),
        in_specs=[pl.BlockSpec((1, tm, d), lambda bi, ti: (bi, ti, 0)),
                  pl.BlockSpec((1, 3, d), lambda bi, ti: (bi, 0, 0)),
                  full(1, d),
                  full(2 * ATTN_WIDTH, d),
                  full(d, ATTN_WIDTH),
                  pl.BlockSpec((ROT_HALF, tm), lambda bi, ti: (0, ti)),
                  pl.BlockSpec((ROT_HALF, tm), lambda bi, ti: (0, ti)),
                  pl.BlockSpec((tm, 128), lambda bi, ti: (ti, 0)),
                  pl.BlockSpec((tm, 128), lambda bi, ti: (ti, 0))],
        out_specs=[pl.BlockSpec((1, nbt, ATTN_WIDTH, MOBA_BLOCK), lambda bi, ti: (bi, ti, 0, 0)),
                   pl.BlockSpec((1, nbt, MOBA_BLOCK, ATTN_WIDTH), lambda bi, ti: (bi, ti, 0, 0)),
                   pl.BlockSpec((1, nbt, ATTN_WIDTH, MOBA_BLOCK), lambda bi, ti: (bi, ti, 0, 0)),
                   pl.BlockSpec((1, nbt, 1, ATTN_WIDTH), lambda bi, ti: (bi, ti, 0, 0))],
        out_shape=[jax.ShapeDtypeStruct((b, nb, ATTN_WIDTH, MOBA_BLOCK), _BF16),
                   jax.ShapeDtypeStruct((b, nb, MOBA_BLOCK, ATTN_WIDTH), _BF16),
                   jax.ShapeDtypeStruct((b, nb, ATTN_WIDTH, MOBA_BLOCK), _BF16),
                   jax.ShapeDtypeStruct((b, nb, 1, ATTN_WIDTH), _F32)],
        compiler_params=pltpu.CompilerParams(dimension_semantics=("arbitrary", "arbitrary"),
                                             vmem_limit_bytes=VMEM_LIMIT_BYTES),
        name="qkv_proj",
    )(x, mod, g_norm, wqv_t, wk, cos_t, sin_t, cos_l, sin_l)


V_EXT = HEAD_DIM + 16


def _attn_kernel(qt_ref, k_ref, vt_ref, km_ref, o_ref, qh_ref, sel_ref, acc_ref, m_ref):
    i = pl.program_id(1)
    nb = k_ref.shape[1]
    blk = MOBA_BLOCK
    neg_inf = -jnp.inf

    q_t = qt_ref[0, 0]
    km = km_ref[0].astype(_BF16)
    k_own = k_ref[0, i]
    v_own = vt_ref[0, i]
    pair_row = lax.broadcasted_iota(jnp.int32, (2 * HEAD_DIM, blk), 0)
    blk_id = lax.broadcasted_iota(jnp.int32, (nb, blk), 0)
    key_pos = lax.broadcasted_iota(jnp.int32, (blk, blk), 0)
    qry_pos = lax.broadcasted_iota(jnp.int32, (blk, blk), 1)
    ones = jnp.ones((V_EXT - HEAD_DIM, blk), _BF16)

    def pair_cols(arr, hd):
        p0 = (hd // 2) * 2 * HEAD_DIM
        return arr[:, p0:p0 + 2 * HEAD_DIM]

    def v_ext(v_blk, hd):
        return jnp.concatenate([v_blk[hd * HEAD_DIM:(hd + 1) * HEAD_DIM], ones], axis=0)

    for hd in range(N_HEADS):
        p0 = (hd // 2) * 2 * HEAD_DIM
        in_head = (pair_row >= HEAD_DIM) == (hd % 2 == 1)
        qh = jnp.where(in_head, q_t[p0:p0 + 2 * HEAD_DIM], jnp.zeros((), _BF16))
        qh_ref[hd] = qh

        gate = jnp.dot(pair_cols(km, hd), qh, preferred_element_type=_F32)
        gate = jnp.where(blk_id < i, gate, neg_inf)
        sel = jnp.zeros((nb, blk), _F32)
        for _ in range(MOBA_TOPK):
            best = jnp.max(gate, axis=0, keepdims=True)
            is_best = (gate == best) & (gate > neg_inf)
            idx = jnp.min(jnp.where(is_best, blk_id, nb), axis=0, keepdims=True)
            pick = blk_id == idx
            sel = jnp.where(pick, 1.0, sel)
            gate = jnp.where(pick, neg_inf, gate)
        sel_ref[hd] = sel

        s = jnp.dot(pair_cols(k_own, hd), qh, preferred_element_type=_F32)
        s = jnp.where(key_pos <= qry_pos, s, neg_inf)
        m = jnp.max(s, axis=0, keepdims=True)
        p = jnp.exp(s - m).astype(_BF16)
        acc_ref[hd] = jnp.dot(v_ext(v_own, hd), p, preferred_element_type=_F32)
        m_ref[hd:hd + 1, :] = m

    def past_block(j, carry):
        k_j = k_ref[0, j]
        v_j = vt_ref[0, j]
        for hd in range(N_HEADS):
            s = jnp.dot(pair_cols(k_j, hd), qh_ref[hd], preferred_element_type=_F32)
            chosen = sel_ref[hd, pl.ds(j, 1), :] > 0.5
            m_old = m_ref[hd:hd + 1, :]
            m_new = jnp.where(chosen, jnp.maximum(m_old, jnp.max(s, axis=0, keepdims=True)), m_old)
            shift = jnp.where(chosen, m_new, jnp.inf)
            p = jnp.exp(s - shift).astype(_BF16)
            acc_ref[hd] = (jnp.exp(m_old - m_new) * acc_ref[hd]
                           + jnp.dot(v_ext(v_j, hd), p, preferred_element_type=_F32))
            m_ref[hd:hd + 1, :] = m_new
        return carry

    lax.fori_loop(0, i, past_block, 0)

    outs = []
    for hd in range(N_HEADS):
        acc = acc_ref[hd]
        outs.append(acc[:HEAD_DIM] / acc[HEAD_DIM:HEAD_DIM + 1])
    o_ref[0] = jnp.concatenate(outs, axis=0).T.astype(o_ref.dtype)


def _moba_attention(q_t, k, v_t, kmean):
    b, nb = k.shape[0], k.shape[1]
    s = nb * MOBA_BLOCK
    return pl.pallas_call(
        _attn_kernel,
        grid=(b, nb),
        in_specs=[pl.BlockSpec((1, 1, ATTN_WIDTH, MOBA_BLOCK), lambda bi, qi: (bi, qi, 0, 0)),
                  pl.BlockSpec((1, nb, MOBA_BLOCK, ATTN_WIDTH), lambda bi, qi: (bi, 0, 0, 0)),
                  pl.BlockSpec((1, nb, ATTN_WIDTH, MOBA_BLOCK), lambda bi, qi: (bi, 0, 0, 0)),
                  pl.BlockSpec((1, nb, ATTN_WIDTH), lambda bi, qi: (bi, 0, 0))],
        out_specs=pl.BlockSpec((1, MOBA_BLOCK, ATTN_WIDTH), lambda bi, qi: (bi, qi, 0)),
        out_shape=jax.ShapeDtypeStruct((b, s, ATTN_WIDTH), _BF16),
        scratch_shapes=[pltpu.VMEM((N_HEADS, 2 * HEAD_DIM, MOBA_BLOCK), _BF16),
                        pltpu.VMEM((N_HEADS, nb, MOBA_BLOCK), _F32),
                        pltpu.VMEM((N_HEADS, V_EXT, MOBA_BLOCK), _F32),
                        pltpu.VMEM((N_HEADS, MOBA_BLOCK), _F32)],
        compiler_params=pltpu.CompilerParams(dimension_semantics=("arbitrary", "arbitrary"),
                                             vmem_limit_bytes=VMEM_LIMIT_BYTES),
        name="moba_attention",
    )(q_t, k, v_t, kmean)


_U0, _ZP0, _ZA0, _MP0, _MA0, _REST_W = 0, 512, 1024, 1536, 2560, 3584


def _out_kernel(x_ref, mod_ref, g_ref, gf_ref, wr_ref, wg_ref, ps_ref, wpu_ref, wau_ref, wo_ref, att_ref,
                out_ref, hist_ref, carry_ref):
    t = pl.program_id(1)
    tm = x_ref.shape[1]
    x = x_ref[0]
    h = _adaln_norm(x, g_ref[...], mod_ref).astype(_BF16)
    proj = jnp.dot(h, wr_ref[...], preferred_element_type=_F32)
    u = proj[:, _U0:_U0 + POOL_WIDTH]
    z_pool = proj[:, _ZP0:_ZP0 + POOL_WIDTH]
    z_attn = proj[:, _ZA0:_ZA0 + ATTN_WIDTH]
    m_pool = proj[:, _MP0:_MP0 + D_MODEL]
    m_attn = proj[:, _MA0:_MA0 + D_MODEL]

    @pl.when(t == 0)
    def _():
        carry_ref[...] = jnp.zeros_like(carry_ref)

    hist_ref[0:POOL_HIST, :] = carry_ref[...]
    hist_ref[POOL_HIST:POOL_HIST + tm, :] = u
    carry_ref[...] = u[tm - POOL_HIST:tm, :]
    pos = t * tm + lax.broadcasted_iota(jnp.int32, (tm, POOL_GC), 0)
    ys = []
    for g, w in enumerate(POOL_WINDOWS):
        lanes = slice(g * POOL_GC, (g + 1) * POOL_GC)
        acc = hist_ref[POOL_HIST:POOL_HIST + tm, lanes]
        for back in range(1, w):
            acc = acc + hist_ref[POOL_HIST - back:POOL_HIST - back + tm, lanes]
        cnt = jnp.minimum(pos + 1, w).astype(_F32)
        d = acc / cnt - u[:, lanes]
        ys.append(jnp.dot(d.astype(_BF16), wg_ref[g], preferred_element_type=_F32))
    y_pool = jnp.concatenate(ys, axis=1) * ps_ref[...] * jax.nn.silu(z_pool)
    y_pool = jnp.dot(y_pool.astype(_BF16), wpu_ref[...], preferred_element_type=_F32)

    y_attn = att_ref[0].astype(_F32) * jax.nn.silu(z_attn)
    y_attn = jnp.dot(y_attn.astype(_BF16), wau_ref[...], preferred_element_type=_F32)

    merged = jax.nn.sigmoid(m_pool) * y_pool + jax.nn.sigmoid(m_attn) * y_attn
    r = jnp.dot(merged.astype(_BF16), wo_ref[...], preferred_element_type=_F32)
    xn = x + mod_ref[0, 2:3, :] * r
    ms = jnp.mean(xn * xn, axis=-1, keepdims=True)
    out_ref[0] = xn * lax.rsqrt(ms + EPS) * gf_ref[...]


def _pool_merge_out(x, mod, g_norm, g_final, w_rest, w_grp, pool_scale, w_pool_up, w_attn_up, w_out, att):
    b, s, d = x.shape
    tm = TM_OUT
    full = lambda *shape: pl.BlockSpec(shape, lambda bi, ti: (0,) * len(shape))
    return pl.pallas_call(
        _out_kernel,
        grid=(b, s // tm),
        in_specs=[pl.BlockSpec((1, tm, d), lambda bi, ti: (bi, ti, 0)),
                  pl.BlockSpec((1, 3, d), lambda bi, ti: (bi, 0, 0)),
                  full(1, d), full(1, d),
                  full(d, _REST_W),
                  full(len(POOL_WINDOWS), POOL_GC, POOL_GC),
                  full(1, POOL_WIDTH),
                  full(POOL_WIDTH, d), full(ATTN_WIDTH, d), full(d, d),
                  pl.BlockSpec((1, tm, ATTN_WIDTH), lambda bi, ti: (bi, ti, 0))],
        out_specs=pl.BlockSpec((1, tm, d), lambda bi, ti: (bi, ti, 0)),
        out_shape=jax.ShapeDtypeStruct((b, s, d), _F32),
        scratch_shapes=[pltpu.VMEM((POOL_HIST + tm, POOL_WIDTH), _F32),
                        pltpu.VMEM((POOL_HIST, POOL_WIDTH), _F32)],
        compiler_params=pltpu.CompilerParams(dimension_semantics=("arbitrary", "arbitrary"),
                                             vmem_limit_bytes=VMEM_LIMIT_BYTES),
        name="pool_merge_out",
    )(x, mod, g_norm, g_final, w_rest, w_grp, pool_scale, w_pool_up, w_attn_up, w_out, att)


def _rope_tables(s):
    pos = jnp.arange(s, dtype=_F32)
    inv_freq = ROPE_THETA ** (-jnp.arange(0, ROT_DIM, 2, dtype=_F32) / ROT_DIM)
    ang = pos[:, None] * inv_freq[None, :]
    cos, sin = jnp.cos(ang), jnp.sin(ang)
    pad_one = jnp.ones((s, HEAD_DIM - ROT_DIM), _F32)
    pad_zero = jnp.zeros((s, HEAD_DIM - ROT_DIM), _F32)
    cos_l = jnp.concatenate([cos, cos, pad_one] * 2, axis=1)
    sin_l = jnp.concatenate([-sin, sin, pad_zero] * 2, axis=1)
    return cos.T, sin.T, cos_l, sin_l


def kernel(x, c, w_ada, b_ada, g_norm, w_in, w_pool_grp, pool_scale, w_pool_up, w_attn_up, w_out, g_final):
    b, s, d = x.shape
    assert w_ada.shape[0] == 1, "single-layer configuration only"
    assert d == D_MODEL and s % TM_QKV == 0 and s % TM_OUT == 0
    w = w_in[0]
    o_q, o_k, o_v, o_za, o_mp = 2 * POOL_WIDTH, 2 * POOL_WIDTH + ATTN_WIDTH, 2 * POOL_WIDTH + 2 * ATTN_WIDTH, \
        2 * POOL_WIDTH + 3 * ATTN_WIDTH, 2 * POOL_WIDTH + 4 * ATTN_WIDTH
    wqv_t = jnp.concatenate([w[:, o_q:o_k], w[:, o_v:o_za]], axis=1).T.astype(_BF16)
    wk = w[:, o_k:o_v].astype(_BF16)
    w_rest = jnp.concatenate([w[:, :o_q], w[:, o_za:]], axis=1).astype(_BF16)

    mod = _adaln_mod(c, w_ada[0], b_ada[0])
    cos_t, sin_t, cos_l, sin_l = _rope_tables(s)
    g1 = g_norm[0].reshape(1, d)
    q_t, k, v_t, kmean = _qkv_proj(x, mod, g1, wqv_t, wk, cos_t, sin_t, cos_l, sin_l)
    att = _moba_attention(q_t, k, v_t, kmean.reshape(b, s // MOBA_BLOCK, ATTN_WIDTH))
    return _pool_merge_out(x, mod, g1, g_final.reshape(1, d), w_rest,
                           w_pool_grp[0].astype(_BF16), pool_scale[0].reshape(1, POOL_WIDTH),
                           w_pool_up[0].astype(_BF16), w_attn_up[0].astype(_BF16), w_out[0].astype(_BF16), att)
```

```python
import functools

import jax
import jax.numpy as jnp
from jax import lax
from jax.experimental import pallas as pl
from jax.experimental.pallas import tpu as pltpu

D_MODEL = 1024
POOL_WIDTH = 512
POOL_WINDOWS = (2, 4, 8, 16)
POOL_GC = 128
POOL_HIST = 16
HEAD_DIM = 64
N_HEADS = 8
ATTN_WIDTH = N_HEADS * HEAD_DIM
MOBA_BLOCK = 256
MOBA_TOPK = 3
ROT_DIM = 16
ROT_HALF = ROT_DIM // 2
ROPE_THETA = 500000.0
EPS = 1e-6

TM_QKV = 512
TM_OUT = 256
VMEM_LIMIT_BYTES = 56 * 1024 * 1024

_BF16 = jnp.bfloat16
_F32 = jnp.float32
_NT_DIMS = (((1,), (1,)), ((), ()))


def _adaln_norm(x, g, mod_ref):
    ms = jnp.mean(x * x, axis=-1, keepdims=True)
    y = x * lax.rsqrt(ms + EPS) * g
    shift = mod_ref[0, 0:1, :]
    scl = mod_ref[0, 1:2, :]
    return y * (1.0 + scl) + shift


def _mod_kernel(c_ref, w_ref, b_ref, o_ref):
    o_ref[...] = jnp.dot(c_ref[...], w_ref[...], preferred_element_type=_F32,
                         precision=lax.Precision.HIGHEST) + b_ref[...]


def _adaln_mod(c, w_ada, b_ada):
    nb_, d = c.shape
    rows = -(-nb_ // 8) * 8
    c_pad = jnp.zeros((rows, d), _F32).at[:nb_].set(c)
    n_out = w_ada.shape[1]
    out = pl.pallas_call(
        _mod_kernel,
        grid=(n_out // d,),
        in_specs=[pl.BlockSpec((rows, d), lambda j: (0, 0)),
                  pl.BlockSpec((d, d), lambda j: (0, j)),
                  pl.BlockSpec((1, d), lambda j: (0, j))],
        out_specs=pl.BlockSpec((rows, d), lambda j: (0, j)),
        out_shape=jax.ShapeDtypeStruct((rows, n_out), _F32),
        name="adaln_mod",
    )(c_pad, w_ada, b_ada.reshape(1, n_out))
    return out[:nb_].reshape(nb_, n_out // d, d)


def _qkv_kernel(x_ref, mod_ref, g_ref, wqv_ref, wk_ref, cos_t_ref, sin_t_ref, cos_l_ref, sin_l_ref,
                qt_ref, k_ref, vt_ref, km_ref):
    tm = x_ref.shape[1]
    h = _adaln_norm(x_ref[0], g_ref[...], mod_ref).astype(_BF16)

    qv_t = lax.dot_general(wqv_ref[...], h, _NT_DIMS, preferred_element_type=_F32)
    cos_t = cos_t_ref[...]
    sin_t = sin_t_ref[...]
    pieces = []
    for hd in range(N_HEADS):
        base = hd * HEAD_DIM
        x1 = qv_t[base:base + ROT_HALF]
        x2 = qv_t[base + ROT_HALF:base + ROT_DIM]
        pieces += [x1 * cos_t - x2 * sin_t, x2 * cos_t + x1 * sin_t, qv_t[base + ROT_DIM:base + HEAD_DIM]]
    q_t = (jnp.concatenate(pieces, axis=0) * (HEAD_DIM ** -0.5 * LOG2_E)).astype(_BF16)
    v_t = qv_t[ATTN_WIDTH:].astype(_BF16)

    k = jnp.dot(h, wk_ref[...], preferred_element_type=_F32)
    cos_l = jnp.concatenate([cos_l_ref[...]] * (ATTN_WIDTH // 128), axis=1)
    sin_l = jnp.concatenate([sin_l_ref[...]] * (ATTN_WIDTH // 128), axis=1)
    dim = lax.broadcasted_iota(jnp.int32, k.shape, 1) % HEAD_DIM
    partner = jnp.where(dim < ROT_HALF,
                        pltpu.roll(k, ATTN_WIDTH - ROT_HALF, axis=1),
                        pltpu.roll(k, ROT_HALF, axis=1))
    k = jnp.where(dim < ROT_DIM, k * cos_l + partner * sin_l, k)

    for i in range(tm // MOBA_BLOCK):
        rows = slice(i * MOBA_BLOCK, (i + 1) * MOBA_BLOCK)
        kb = k[rows]
        km_ref[0, i] = jnp.mean(kb, axis=0, keepdims=True)
        k_ref[0, i] = kb.astype(_BF16)
        qt_ref[0, i] = q_t[:, rows]
        vt_ref[0, i] = v_t[:, rows]


def _qkv_proj(x, mod, g_norm, wqv_t, wk, cos_t, sin_t, cos_l, sin_l):
    b, s, d = x.shape
    nb = s // MOBA_BLOCK
    tm = TM_QKV
    nbt = tm // MOBA_BLOCK
    full = lambda *shape: pl.BlockSpec(shape, lambda bi, ti: (0,) * len(shape))
    return pl.pallas_call(
        _qkv_kernel,
        grid=(b, s // tm),
        in_specs=[pl.BlockSpec((1, tm, d), lambda bi, ti: (bi, ti, 0)),
                  pl.BlockSpec((1, 3, d), lambda bi, ti: (bi, 0, 0)),
                  full(1, d),
                  full(2 * ATTN_WIDTH, d),
                  full(d, ATTN_WIDTH),
                  pl.BlockSpec((ROT_HALF, tm), lambda bi, ti: (0, ti)),
                  pl.BlockSpec((ROT_HALF, tm), lambda bi, ti: (0, ti)),
                  pl.BlockSpec((tm, 128), lambda bi, ti: (ti, 0)),
                  pl.BlockSpec((tm, 128), lambda bi, ti: (ti, 0))],
        out_specs=[pl.BlockSpec((1, nbt, ATTN_WIDTH, MOBA_BLOCK), lambda bi, ti: (bi, ti, 0, 0)),
                   pl.BlockSpec((1, nbt, MOBA_BLOCK, ATTN_WIDTH), lambda bi, ti: (bi, ti, 0, 0)),
                   pl.BlockSpec((1, nbt, ATTN_WIDTH, MOBA_BLOCK), lambda bi, ti: (bi, ti, 0, 0)),
                   pl.BlockSpec((1, nbt, 1, ATTN_WIDTH), lambda bi, ti: (bi, ti, 0, 0))],
        out_shape=[jax.ShapeDtypeStruct((b, nb, ATTN_WIDTH, MOBA_BLOCK), _BF16),
                   jax.ShapeDtypeStruct((b, nb, MOBA_BLOCK, ATTN_WIDTH), _BF16),
                   jax.ShapeDtypeStruct((b, nb, ATTN_WIDTH, MOBA_BLOCK), _BF16),
                   jax.ShapeDtypeStruct((b, nb, 1, ATTN_WIDTH), _F32)],
        compiler_params=pltpu.CompilerParams(dimension_semantics=("arbitrary", "arbitrary"),
                                             vmem_limit_bytes=VMEM_LIMIT_BYTES),
        name="qkv_proj",
    )(x, mod, g_norm, wqv_t, wk, cos_t, sin_t, cos_l, sin_l)


V_EXT = HEAD_DIM + 16
QK_AHEAD = 4
KV_UNROLL = 4
LOG2_E = 1.4426950408889634


def _attn_kernel(qt_ref, k_ref, vt_ref, km_ref, o_ref, qh_ref, sel_ref, acc_ref):
    i = pl.program_id(1)
    nb = k_ref.shape[1]
    blk = MOBA_BLOCK
    neg_inf = -jnp.inf

    q_t = qt_ref[0, 0]
    km = km_ref[0].astype(_BF16)
    k_own = k_ref[0, i]
    v_own = vt_ref[0, i]
    pair_row = lax.broadcasted_iota(jnp.int32, (2 * HEAD_DIM, blk), 0)
    blk_id = lax.broadcasted_iota(jnp.int32, (nb, blk), 0)
    key_pos = lax.broadcasted_iota(jnp.int32, (blk, blk), 0)
    qry_pos = lax.broadcasted_iota(jnp.int32, (blk, blk), 1)
    ones = jnp.ones((V_EXT - HEAD_DIM, blk), _BF16)

    def pair_cols(arr, hd):
        p0 = (hd // 2) * 2 * HEAD_DIM
        return arr[:, p0:p0 + 2 * HEAD_DIM]

    def v_ext(v_blk, hd):
        return jnp.concatenate([v_blk[hd * HEAD_DIM:(hd + 1) * HEAD_DIM], ones], axis=0)

    def issue_ahead(items, produce, consume):
        pending = [produce(it) for it in items[:QK_AHEAD]]
        for n, it in enumerate(items):
            val = pending.pop(0)
            if n + QK_AHEAD < len(items):
                pending.append(produce(items[n + QK_AHEAD]))
            consume(it, val)

    heads = list(range(N_HEADS))
    for hd in heads:
        p0 = (hd // 2) * 2 * HEAD_DIM
        in_head = (pair_row >= HEAD_DIM) == (hd % 2 == 1)
        qh_ref[hd] = jnp.where(in_head, q_t[p0:p0 + 2 * HEAD_DIM], jnp.zeros((), _BF16))

    def gate_scores(hd):
        return jnp.dot(pair_cols(km, hd), qh_ref[hd], preferred_element_type=_F32)

    def select_blocks(hd, gate):
        gate = jnp.where(blk_id < i, gate, neg_inf)
        sel = jnp.zeros((nb, blk), _F32)
        for _ in range(MOBA_TOPK):
            best = jnp.max(gate, axis=0, keepdims=True)
            is_best = (gate == best) & (gate > neg_inf)
            idx = jnp.min(jnp.where(is_best, blk_id, nb), axis=0, keepdims=True)
            pick = blk_id == idx
            sel = jnp.where(pick, 1.0, sel)
            gate = jnp.where(pick, neg_inf, gate)
        sel_ref[hd] = sel

    issue_ahead(heads, gate_scores, select_blocks)

    m_own = [None] * N_HEADS

    def own_scores(hd):
        return jnp.dot(pair_cols(k_own, hd), qh_ref[hd], preferred_element_type=_F32)

    def own_softmax(hd, s):
        s = jnp.where(key_pos <= qry_pos, s, neg_inf)
        m = jnp.max(s, axis=0, keepdims=True)
        p = jnp.exp2(s - m).astype(_BF16)
        acc_ref[hd] = jnp.dot(v_ext(v_own, hd), p, preferred_element_type=_F32)
        m_own[hd] = m

    issue_ahead(heads, own_scores, own_softmax)

    def past_blocks(t, m_run):
        j0 = t * KV_UNROLL
        k_j = [k_ref[0, j0 + u] for u in range(KV_UNROLL)]
        v_j = [vt_ref[0, j0 + u] for u in range(KV_UNROLL)]
        m_cur = list(m_run)

        def scores(item):
            u, hd = item
            return jnp.dot(pair_cols(k_j[u], hd), qh_ref[hd], preferred_element_type=_F32)

        def softmax_pv(item, s):
            u, hd = item
            chosen = sel_ref[hd, pl.ds(j0 + u, 1), :] > 0.5
            m_old = m_cur[hd]
            m_new = jnp.where(chosen, jnp.maximum(m_old, jnp.max(s, axis=0, keepdims=True)), m_old)
            shift = jnp.where(chosen, m_new, jnp.inf)
            p = jnp.exp2(s - shift).astype(_BF16)
            acc_ref[hd] = (jnp.exp2(m_old - m_new) * acc_ref[hd]
                           + jnp.dot(v_ext(v_j[u], hd), p, preferred_element_type=_F32))
            m_cur[hd] = m_new

        issue_ahead([(u, hd) for u in range(KV_UNROLL) for hd in heads], scores, softmax_pv)
        return tuple(m_cur)

    lax.fori_loop(0, (i + KV_UNROLL - 1) // KV_UNROLL, past_blocks, tuple(m_own))

    outs = []
    for hd in range(N_HEADS):
        acc = acc_ref[hd]
        outs.append(acc[:HEAD_DIM] / acc[HEAD_DIM:HEAD_DIM + 1])
    o_ref[0] = jnp.concatenate(outs, axis=0).T.astype(o_ref.dtype)


def _moba_attention(q_t, k, v_t, kmean):
    b, nb = k.shape[0], k.shape[1]
    s = nb * MOBA_BLOCK
    assert nb % KV_UNROLL == 0
    return pl.pallas_call(
        _attn_kernel,
        grid=(b, nb),
        in_specs=[pl.BlockSpec((1, 1, ATTN_WIDTH, MOBA_BLOCK), lambda bi, qi: (bi, qi, 0, 0)),
                  pl.BlockSpec((1, nb, MOBA_BLOCK, ATTN_WIDTH), lambda bi, qi: (bi, 0, 0, 0)),
                  pl.BlockSpec((1, nb, ATTN_WIDTH, MOBA_BLOCK), lambda bi, qi: (bi, 0, 0, 0)),
                  pl.BlockSpec((1, nb, ATTN_WIDTH), lambda bi, qi: (bi, 0, 0))],
        out_specs=pl.BlockSpec((1, MOBA_BLOCK, ATTN_WIDTH), lambda bi, qi: (bi, qi, 0)),
        out_shape=jax.ShapeDtypeStruct((b, s, ATTN_WIDTH), _BF16),
        scratch_shapes=[pltpu.VMEM((N_HEADS, 2 * HEAD_DIM, MOBA_BLOCK), _BF16),
                        pltpu.VMEM((N_HEADS, nb, MOBA_BLOCK), _F32),
                        pltpu.VMEM((N_HEADS, V_EXT, MOBA_BLOCK), _F32)],
        compiler_params=pltpu.CompilerParams(dimension_semantics=("arbitrary", "arbitrary"),
                                             vmem_limit_bytes=VMEM_LIMIT_BYTES),
        name="moba_attention",
    )(q_t, k, v_t, kmean)


_U0, _ZP0, _ZA0, _MP0, _MA0, _REST_W = 0, 512, 1024, 1536, 2560, 3584


def _out_kernel(x_ref, mod_ref, g_ref, gf_ref, wr_ref, wg_ref, ps_ref, wpu_ref, wau_ref, wo_ref, att_ref,
                out_ref, hist_ref, carry_ref):
    t = pl.program_id(1)
    tm = x_ref.shape[1]
    x = x_ref[0]
    h = _adaln_norm(x, g_ref[...], mod_ref).astype(_BF16)
    proj = jnp.dot(h, wr_ref[...], preferred_element_type=_F32)
    u = proj[:, _U0:_U0 + POOL_WIDTH]
    z_pool = proj[:, _ZP0:_ZP0 + POOL_WIDTH]
    z_attn = proj[:, _ZA0:_ZA0 + ATTN_WIDTH]
    m_pool = proj[:, _MP0:_MP0 + D_MODEL]
    m_attn = proj[:, _MA0:_MA0 + D_MODEL]

    @pl.when(t == 0)
    def _():
        carry_ref[...] = jnp.zeros_like(carry_ref)

    hist_ref[0:POOL_HIST, :] = carry_ref[...]
    hist_ref[POOL_HIST:POOL_HIST + tm, :] = u
    carry_ref[...] = u[tm - POOL_HIST:tm, :]
    pos = t * tm + lax.broadcasted_iota(jnp.int32, (tm, POOL_GC), 0)
    ys = []
    for g, w in enumerate(POOL_WINDOWS):
        lanes = slice(g * POOL_GC, (g + 1) * POOL_GC)
        acc = hist_ref[POOL_HIST:POOL_HIST + tm, lanes]
        for back in range(1, w):
            acc = acc + hist_ref[POOL_HIST - back:POOL_HIST - back + tm, lanes]
        cnt = jnp.minimum(pos + 1, w).astype(_F32)
        d = acc / cnt - u[:, lanes]
        ys.append(jnp.dot(d.astype(_BF16), wg_ref[g], preferred_element_type=_F32))
    y_pool = jnp.concatenate(ys, axis=1) * ps_ref[...] * jax.nn.silu(z_pool)
    y_pool = jnp.dot(y_pool.astype(_BF16), wpu_ref[...], preferred_element_type=_F32)

    y_attn = att_ref[0].astype(_F32) * jax.nn.silu(z_attn)
    y_attn = jnp.dot(y_attn.astype(_BF16), wau_ref[...], preferred_element_type=_F32)

    merged = jax.nn.sigmoid(m_pool) * y_pool + jax.nn.sigmoid(m_attn) * y_attn
    r = jnp.dot(merged.astype(_BF16), wo_ref[...], preferred_element_type=_F32)
    xn = x + mod_ref[0, 2:3, :] * r
    ms = jnp.mean(xn * xn, axis=-1, keepdims=True)
    out_ref[0] = xn * lax.rsqrt(ms + EPS) * gf_ref[...]


def _pool_merge_out(x, mod, g_norm, g_final, w_rest, w_grp, pool_scale, w_pool_up, w_attn_up, w_out, att):
    b, s, d = x.shape
    tm = TM_OUT
    full = lambda *shape: pl.BlockSpec(shape, lambda bi, ti: (0,) * len(shape))
    return pl.pallas_call(
        _out_kernel,
        grid=(b, s // tm),
        in_specs=[pl.BlockSpec((1, tm, d), lambda bi, ti: (bi, ti, 0)),
                  pl.BlockSpec((1, 3, d), lambda bi, ti: (bi, 0, 0)),
                  full(1, d), full(1, d),
                  full(d, _REST_W),
                  full(len(POOL_WINDOWS), POOL_GC, POOL_GC),
                  full(1, POOL_WIDTH),
                  full(POOL_WIDTH, d), full(ATTN_WIDTH, d), full(d, d),
                  pl.BlockSpec((1, tm, ATTN_WIDTH), lambda bi, ti: (bi, ti, 0))],
        out_specs=pl.BlockSpec((1, tm, d), lambda bi, ti: (bi, ti, 0)),
        out_shape=jax.ShapeDtypeStruct((b, s, d), _F32),
        scratch_shapes=[pltpu.VMEM((POOL_HIST + tm, POOL_WIDTH), _F32),
                        pltpu.VMEM((POOL_HIST, POOL_WIDTH), _F32)],
        compiler_params=pltpu.CompilerParams(dimension_semantics=("arbitrary", "arbitrary"),
                                             vmem_limit_bytes=VMEM_LIMIT_BYTES),
        name="pool_merge_out",
    )(x, mod, g_norm, g_final, w_rest, w_grp, pool_scale, w_pool_up, w_attn_up, w_out, att)


def _rope_tables(s):
    pos = jnp.arange(s, dtype=_F32)
    inv_freq = ROPE_THETA ** (-jnp.arange(0, ROT_DIM, 2, dtype=_F32) / ROT_DIM)
    ang = pos[:, None] * inv_freq[None, :]
    cos, sin = jnp.cos(ang), jnp.sin(ang)
    pad_one = jnp.ones((s, HEAD_DIM - ROT_DIM), _F32)
    pad_zero = jnp.zeros((s, HEAD_DIM - ROT_DIM), _F32)
    cos_l = jnp.concatenate([cos, cos, pad_one] * 2, axis=1)
    sin_l = jnp.concatenate([-sin, sin, pad_zero] * 2, axis=1)
    return cos.T, sin.T, cos_l, sin_l


def kernel(x, c, w_ada, b_ada, g_norm, w_in, w_pool_grp, pool_scale, w_pool_up, w_attn_up, w_out, g_final):
    b, s, d = x.shape
    assert w_ada.shape[0] == 1, "single-layer configuration only"
    assert d == D_MODEL and s % TM_QKV == 0 and s % TM_OUT == 0
    w = w_in[0]
    o_q, o_k, o_v, o_za, o_mp = 2 * POOL_WIDTH, 2 * POOL_WIDTH + ATTN_WIDTH, 2 * POOL_WIDTH + 2 * ATTN_WIDTH, \
        2 * POOL_WIDTH + 3 * ATTN_WIDTH, 2 * POOL_WIDTH + 4 * ATTN_WIDTH
    wqv_t = jnp.concatenate([w[:, o_q:o_k], w[:, o_v:o_za]], axis=1).T.astype(_BF16)
    wk = w[:, o_k:o_v].astype(_BF16)
    w_rest = jnp.concatenate([w[:, :o_q], w[:, o_za:]], axis=1).astype(_BF16)

    mod = _adaln_mod(c, w_ada[0], b_ada[0])
    cos_t, sin_t, cos_l, sin_l = _rope_tables(s)
    g1 = g_norm[0].reshape(1, d)
    q_t, k, v_t, kmean = _qkv_proj(x, mod, g1, wqv_t, wk, cos_t, sin_t, cos_l, sin_l)
    att = _moba_attention(q_t, k, v_t, kmean.reshape(b, s // MOBA_BLOCK, ATTN_WIDTH))
    return _pool_merge_out(x, mod, g1, g_final.reshape(1, d), w_rest,
                           w_pool_grp[0].astype(_BF16), pool_scale[0].reshape(1, POOL_WIDTH),
                           w_pool_up[0].astype(_BF16), w_attn_up[0].astype(_BF16), w_out[0].astype(_BF16), att)
```

```python
import functools

import jax
import jax.numpy as jnp
from jax import lax
from jax.experimental import pallas as pl
from jax.experimental.pallas import tpu as pltpu

D_MODEL = 1024
POOL_WIDTH = 512
POOL_WINDOWS = (2, 4, 8, 16)
POOL_GC = 128
POOL_HIST = 16
HEAD_DIM = 64
N_HEADS = 8
ATTN_WIDTH = N_HEADS * HEAD_DIM
MOBA_BLOCK = 256
MOBA_TOPK = 3
ROT_DIM = 16
ROT_HALF = ROT_DIM // 2
ROPE_THETA = 500000.0
EPS = 1e-6

TM_QKV = 512
TM_OUT = 512
VMEM_LIMIT_BYTES = 56 * 1024 * 1024

_BF16 = jnp.bfloat16
_F32 = jnp.float32
_NT_DIMS = (((1,), (1,)), ((), ()))


def _adaln_norm(x, g, mod_ref):
    ms = jnp.mean(x * x, axis=-1, keepdims=True)
    y = x * lax.rsqrt(ms + EPS) * g
    shift = mod_ref[0, 0:1, :]
    scl = mod_ref[0, 1:2, :]
    return y * (1.0 + scl) + shift


def _mod_kernel(c_ref, w_ref, b_ref, o_ref):
    o_ref[...] = jnp.dot(c_ref[...], w_ref[...], preferred_element_type=_F32,
                         precision=lax.Precision.HIGHEST) + b_ref[...]


def _adaln_mod(c, w_ada, b_ada):
    nb_, d = c.shape
    rows = -(-nb_ // 8) * 8
    c_pad = jnp.zeros((rows, d), _F32).at[:nb_].set(c)
    n_out = w_ada.shape[1]
    out = pl.pallas_call(
        _mod_kernel,
        grid=(n_out // d,),
        in_specs=[pl.BlockSpec((rows, d), lambda j: (0, 0)),
                  pl.BlockSpec((d, d), lambda j: (0, j)),
                  pl.BlockSpec((1, d), lambda j: (0, j))],
        out_specs=pl.BlockSpec((rows, d), lambda j: (0, j)),
        out_shape=jax.ShapeDtypeStruct((rows, n_out), _F32),
        name="adaln_mod",
    )(c_pad, w_ada, b_ada.reshape(1, n_out))
    return out[:nb_].reshape(nb_, n_out // d, d)


def _qkv_kernel(x_ref, mod_ref, g_ref, wqv_ref, wk_ref, cos_t_ref, sin_t_ref, cos_l_ref, sin_l_ref,
                qt_ref, k_ref, vt_ref, km_ref):
    tm = x_ref.shape[1]
    h = _adaln_norm(x_ref[0], g_ref[...], mod_ref).astype(_BF16)

    qv_t = lax.dot_general(wqv_ref[...], h, _NT_DIMS, preferred_element_type=_F32)
    cos_t = cos_t_ref[...]
    sin_t = sin_t_ref[...]
    pieces = []
    for hd in range(N_HEADS):
        base = hd * HEAD_DIM
        x1 = qv_t[base:base + ROT_HALF]
        x2 = qv_t[base + ROT_HALF:base + ROT_DIM]
        pieces += [x1 * cos_t - x2 * sin_t, x2 * cos_t + x1 * sin_t, qv_t[base + ROT_DIM:base + HEAD_DIM]]
    q_t = (jnp.concatenate(pieces, axis=0) * (HEAD_DIM ** -0.5 * LOG2_E)).astype(_BF16)
    v_t = qv_t[ATTN_WIDTH:].astype(_BF16)

    k = jnp.dot(h, wk_ref[...], preferred_element_type=_F32)
    cos_l = jnp.concatenate([cos_l_ref[...]] * (ATTN_WIDTH // 128), axis=1)
    sin_l = jnp.concatenate([sin_l_ref[...]] * (ATTN_WIDTH // 128), axis=1)
    dim = lax.broadcasted_iota(jnp.int32, k.shape, 1) % HEAD_DIM
    partner = jnp.where(dim < ROT_HALF,
                        pltpu.roll(k, ATTN_WIDTH - ROT_HALF, axis=1),
                        pltpu.roll(k, ROT_HALF, axis=1))
    k = jnp.where(dim < ROT_DIM, k * cos_l + partner * sin_l, k)

    for i in range(tm // MOBA_BLOCK):
        rows = slice(i * MOBA_BLOCK, (i + 1) * MOBA_BLOCK)
        kb = k[rows]
        km_ref[0, i] = jnp.mean(kb, axis=0, keepdims=True)
        k_ref[0, i] = kb.astype(_BF16)
        qt_ref[0, i] = q_t[:, rows]
        vt_ref[0, i] = v_t[:, rows]


def _qkv_proj(x, mod, g_norm, wqv_t, wk, cos_t, sin_t, cos_l, sin_l):
    b, s, d = x.shape
    nb = s // MOBA_BLOCK
    tm = TM_QKV
    nbt = tm // MOBA_BLOCK
    full = lambda *shape: pl.BlockSpec(shape, lambda bi, ti: (0,) * len(shape))
    return pl.pallas_call(
        _qkv_kernel,
        grid=(b, s // tm),
        in_specs=[pl.BlockSpec((1, tm, d), lambda bi, ti: (bi, ti, 0)),
                  pl.BlockSpec((1, 3, d), lambda bi, ti: (bi, 0, 0)),
                  full(1, d),
                  full(2 * ATTN_WIDTH, d),
                  full(d, ATTN_WIDTH),
                  pl.BlockSpec((ROT_HALF, tm), lambda bi, ti: (0, ti)),
                  pl.BlockSpec((ROT_HALF, tm), lambda bi, ti: (0, ti)),
                  pl.BlockSpec((tm, 128), lambda bi, ti: (ti, 0)),
                  pl.BlockSpec((tm, 128), lambda bi, ti: (ti, 0))],
        out_specs=[pl.BlockSpec((1, nbt, ATTN_WIDTH, MOBA_BLOCK), lambda bi, ti: (bi, ti, 0, 0)),
                   pl.BlockSpec((1, nbt, MOBA_BLOCK, ATTN_WIDTH), lambda bi, ti: (bi, ti, 0, 0)),
                   pl.BlockSpec((1, nbt, ATTN_WIDTH, MOBA_BLOCK), lambda bi, ti: (bi, ti, 0, 0)),
                   pl.BlockSpec((1, nbt, 1, ATTN_WIDTH), lambda bi, ti: (bi, ti, 0, 0))],
        out_shape=[jax.ShapeDtypeStruct((b, nb, ATTN_WIDTH, MOBA_BLOCK), _BF16),
                   jax.ShapeDtypeStruct((b, nb, MOBA_BLOCK, ATTN_WIDTH), _BF16),
                   jax.ShapeDtypeStruct((b, nb, ATTN_WIDTH, MOBA_BLOCK), _BF16),
                   jax.ShapeDtypeStruct((b, nb, 1, ATTN_WIDTH), _F32)],
        compiler_params=pltpu.CompilerParams(dimension_semantics=("arbitrary", "arbitrary"),
                                             vmem_limit_bytes=VMEM_LIMIT_BYTES),
        name="qkv_proj",
    )(x, mod, g_norm, wqv_t, wk, cos_t, sin_t, cos_l, sin_l)


V_EXT = HEAD_DIM + 16
QK_AHEAD = 4
PV_BEHIND = 1
KV_UNROLL = 2
LOG2_E = 1.4426950408889634


def _attn_kernel(qt_ref, k_ref, vt_ref, km_ref, o_ref, qh_ref, sel_ref, acc_ref, s_ref):
    i = pl.program_id(1)
    nb = k_ref.shape[1]
    blk = MOBA_BLOCK
    neg_inf = -jnp.inf

    q_t = qt_ref[0, 0]
    km = km_ref[0].astype(_BF16)
    k_own = k_ref[0, i]
    v_own = vt_ref[0, i]
    pair_row = lax.broadcasted_iota(jnp.int32, (2 * HEAD_DIM, blk), 0)
    blk_id = lax.broadcasted_iota(jnp.int32, (nb, blk), 0)
    key_pos = lax.broadcasted_iota(jnp.int32, (blk, blk), 0)
    qry_pos = lax.broadcasted_iota(jnp.int32, (blk, blk), 1)
    ones = jnp.ones((V_EXT - HEAD_DIM, blk), _BF16)

    def pair_cols(arr, hd):
        p0 = (hd // 2) * 2 * HEAD_DIM
        return arr[:, p0:p0 + 2 * HEAD_DIM]

    def v_ext(v_blk, hd):
        return jnp.concatenate([v_blk[hd * HEAD_DIM:(hd + 1) * HEAD_DIM], ones], axis=0)

    def software_pipeline(items, score, softmax, accumulate):
        n_items = len(items)
        scores, probs = {}, {}
        for step in range(-QK_AHEAD, n_items + PV_BEHIND):
            if 0 <= step + QK_AHEAD < n_items:
                scores[step + QK_AHEAD] = score(items[step + QK_AHEAD])
            if 0 <= step < n_items:
                probs[step] = softmax(items[step], scores.pop(step))
            if 0 <= step - PV_BEHIND < n_items:
                accumulate(items[step - PV_BEHIND], probs.pop(step - PV_BEHIND))

    heads = list(range(N_HEADS))
    for hd in heads:
        p0 = (hd // 2) * 2 * HEAD_DIM
        in_head = (pair_row >= HEAD_DIM) == (hd % 2 == 1)
        qh_ref[hd] = jnp.where(in_head, q_t[p0:p0 + 2 * HEAD_DIM], jnp.zeros((), _BF16))

    def gate_scores(hd):
        return jnp.dot(pair_cols(km, hd), qh_ref[hd], preferred_element_type=_F32)

    def select_blocks(hd, gate):
        gate = jnp.where(blk_id < i, gate, neg_inf)
        sel = jnp.zeros((nb, blk), _F32)
        for _ in range(MOBA_TOPK):
            best = jnp.max(gate, axis=0, keepdims=True)
            is_best = (gate == best) & (gate > neg_inf)
            idx = jnp.min(jnp.where(is_best, blk_id, nb), axis=0, keepdims=True)
            pick = blk_id == idx
            sel = jnp.where(pick, 1.0, sel)
            gate = jnp.where(pick, neg_inf, gate)
        sel_ref[hd] = sel

    software_pipeline(heads, gate_scores, select_blocks, lambda hd, _: None)

    m_own = [None] * N_HEADS

    def qk(k_blk, hd):
        return jnp.dot(pair_cols(k_blk, hd), qh_ref[hd], preferred_element_type=_F32)

    trip_tiles = [("tile", n) for n in range(KV_UNROLL * N_HEADS)]
    handover = [("next", n) for n in range(QK_AHEAD)]

    def tile_of(n):
        return divmod(n, N_HEADS)

    def own_score(item):
        kind, n = item
        if kind == "own":
            return qk(k_own, n)
        u, hd = tile_of(n)
        return qk(k_ref[0, u], hd)

    def own_softmax(item, s):
        kind, n = item
        if kind == "next":
            s_ref[n] = s
            return None
        s = jnp.where(key_pos <= qry_pos, s, neg_inf)
        m_own[n] = jnp.max(s, axis=0, keepdims=True)
        return jnp.exp2(s - m_own[n]).astype(_BF16)

    def own_accumulate(item, p):
        kind, hd = item
        if kind == "own":
            acc_ref[hd] = jnp.dot(v_ext(v_own, hd), p, preferred_element_type=_F32)

    software_pipeline([("own", hd) for hd in heads] + handover, own_score, own_softmax, own_accumulate)

    def past_blocks(t, m_run):
        j0 = t * KV_UNROLL
        j_next = jnp.minimum(j0 + KV_UNROLL, nb - KV_UNROLL)
        k_j = [k_ref[0, j0 + u] for u in range(KV_UNROLL)]
        v_j = [vt_ref[0, j0 + u] for u in range(KV_UNROLL)]
        m_cur = list(m_run)

        def score(item):
            kind, n = item
            u, hd = tile_of(n)
            if kind == "next":
                return qk(k_ref[0, j_next + u], hd)
            return s_ref[n] if n < QK_AHEAD else qk(k_j[u], hd)

        def softmax(item, s):
            kind, n = item
            if kind == "next":
                s_ref[n] = s
                return None
            u, hd = tile_of(n)
            chosen = sel_ref[hd, pl.ds(j0 + u, 1), :] > 0.5
            m_old = m_cur[hd]
            m_new = jnp.where(chosen, jnp.maximum(m_old, jnp.max(s, axis=0, keepdims=True)), m_old)
            m_cur[hd] = m_new
            shift = jnp.where(chosen, m_new, jnp.inf)
            return jnp.exp2(s - shift).astype(_BF16), jnp.exp2(m_old - m_new)

        def accumulate(item, p_alpha):
            kind, n = item
            if kind == "tile":
                u, hd = tile_of(n)
                p, alpha = p_alpha
                acc_ref[hd] = alpha * acc_ref[hd] + jnp.dot(v_ext(v_j[u], hd), p, preferred_element_type=_F32)

        software_pipeline(trip_tiles + handover, score, softmax, accumulate)
        return tuple(m_cur)

    lax.fori_loop(0, (i + KV_UNROLL - 1) // KV_UNROLL, past_blocks, tuple(m_own))

    outs = []
    for hd in range(N_HEADS):
        acc = acc_ref[hd]
        outs.append(acc[:HEAD_DIM] / acc[HEAD_DIM:HEAD_DIM + 1])
    o_ref[0] = jnp.concatenate(outs, axis=0).T.astype(o_ref.dtype)


def _moba_attention(q_t, k, v_t, kmean):
    b, nb = k.shape[0], k.shape[1]
    s = nb * MOBA_BLOCK
    assert nb % KV_UNROLL == 0
    return pl.pallas_call(
        _attn_kernel,
        grid=(b, nb),
        in_specs=[pl.BlockSpec((1, 1, ATTN_WIDTH, MOBA_BLOCK), lambda bi, qi: (bi, qi, 0, 0)),
                  pl.BlockSpec((1, nb, MOBA_BLOCK, ATTN_WIDTH), lambda bi, qi: (bi, 0, 0, 0)),
                  pl.BlockSpec((1, nb, ATTN_WIDTH, MOBA_BLOCK), lambda bi, qi: (bi, 0, 0, 0)),
                  pl.BlockSpec((1, nb, ATTN_WIDTH), lambda bi, qi: (bi, 0, 0))],
        out_specs=pl.BlockSpec((1, MOBA_BLOCK, ATTN_WIDTH), lambda bi, qi: (bi, qi, 0)),
        out_shape=jax.ShapeDtypeStruct((b, s, ATTN_WIDTH), _BF16),
        scratch_shapes=[pltpu.VMEM((N_HEADS, 2 * HEAD_DIM, MOBA_BLOCK), _BF16),
                        pltpu.VMEM((N_HEADS, nb, MOBA_BLOCK), _F32),
                        pltpu.VMEM((N_HEADS, V_EXT, MOBA_BLOCK), _F32),
                        pltpu.VMEM((QK_AHEAD, MOBA_BLOCK, MOBA_BLOCK), _F32)],
        compiler_params=pltpu.CompilerParams(dimension_semantics=("arbitrary", "arbitrary"),
                                             vmem_limit_bytes=VMEM_LIMIT_BYTES),
        name="moba_attention",
    )(q_t, k, v_t, kmean)


_U0, _ZP0, _ZA0, _MP0, _MA0, _REST_W = 0, 512, 1024, 1536, 2560, 3584
SUB_ROWS = 256


def _out_kernel(x_ref, mod_ref, g_ref, gf_ref, wr_ref, wg_ref, ps_ref, wpu_ref, wau_ref, wo_ref, att_ref,
                out_ref, hist_ref, carry_ref):
    t = pl.program_id(1)
    tm = x_ref.shape[1]
    subs = range(tm // SUB_ROWS)
    rows = [slice(s * SUB_ROWS, (s + 1) * SUB_ROWS) for s in subs]

    def mm(a, w):
        return jnp.dot(a.astype(_BF16), w, preferred_element_type=_F32)

    @pl.when(t == 0)
    def _():
        carry_ref[...] = jnp.zeros_like(carry_ref)

    x = [x_ref[0, r, :] for r in rows]
    h = [_adaln_norm(x[s], g_ref[...], mod_ref).astype(_BF16) for s in subs]
    u = [mm(h[s], wr_ref[:, _U0:_U0 + POOL_WIDTH]) for s in subs]
    z_attn = [mm(h[s], wr_ref[:, _ZA0:_ZA0 + ATTN_WIDTH]) for s in subs]
    z_pool = [mm(h[s], wr_ref[:, _ZP0:_ZP0 + POOL_WIDTH]) for s in subs]
    m_attn = [mm(h[s], wr_ref[:, _MA0:_MA0 + D_MODEL]) for s in subs]
    m_pool = [mm(h[s], wr_ref[:, _MP0:_MP0 + D_MODEL]) for s in subs]

    hist_ref[0:POOL_HIST, :] = carry_ref[...]
    for s in subs:
        hist_ref[POOL_HIST + s * SUB_ROWS:POOL_HIST + (s + 1) * SUB_ROWS, :] = u[s]
    carry_ref[...] = u[-1][SUB_ROWS - POOL_HIST:, :]

    y_attn = [mm(att_ref[0, rows[s], :].astype(_F32) * jax.nn.silu(z_attn[s]), wau_ref[...]) for s in subs]

    y_grp = []
    for s in subs:
        r0 = POOL_HIST + s * SUB_ROWS
        pos = t * tm + s * SUB_ROWS + lax.broadcasted_iota(jnp.int32, (SUB_ROWS, POOL_GC), 0)
        ys = []
        for g, w in enumerate(POOL_WINDOWS):
            lanes = slice(g * POOL_GC, (g + 1) * POOL_GC)
            acc = hist_ref[r0:r0 + SUB_ROWS, lanes]
            for back in range(1, w):
                acc = acc + hist_ref[r0 - back:r0 - back + SUB_ROWS, lanes]
            cnt = jnp.minimum(pos + 1, w).astype(_F32)
            ys.append(mm(acc / cnt - u[s][:, lanes], wg_ref[g]))
        y_grp.append(jnp.concatenate(ys, axis=1))
    y_pool = [mm(y_grp[s] * ps_ref[...] * jax.nn.silu(z_pool[s]), wpu_ref[...]) for s in subs]

    res = [mm(jax.nn.sigmoid(m_pool[s]) * y_pool[s] + jax.nn.sigmoid(m_attn[s]) * y_attn[s], wo_ref[...])
           for s in subs]
    for s in subs:
        xn = x[s] + mod_ref[0, 2:3, :] * res[s]
        ms = jnp.mean(xn * xn, axis=-1, keepdims=True)
        out_ref[0, rows[s], :] = xn * lax.rsqrt(ms + EPS) * gf_ref[...]


def _pool_merge_out(x, mod, g_norm, g_final, w_rest, w_grp, pool_scale, w_pool_up, w_attn_up, w_out, att):
    b, s, d = x.shape
    tm = TM_OUT
    full = lambda *shape: pl.BlockSpec(shape, lambda bi, ti: (0,) * len(shape))
    return pl.pallas_call(
        _out_kernel,
        grid=(b, s // tm),
        in_specs=[pl.BlockSpec((1, tm, d), lambda bi, ti: (bi, ti, 0)),
                  pl.BlockSpec((1, 3, d), lambda bi, ti: (bi, 0, 0)),
                  full(1, d), full(1, d),
                  full(d, _REST_W),
                  full(len(POOL_WINDOWS), POOL_GC, POOL_GC),
                  full(1, POOL_WIDTH),
                  full(POOL_WIDTH, d), full(ATTN_WIDTH, d), full(d, d),
                  pl.BlockSpec((1, tm, ATTN_WIDTH), lambda bi, ti: (bi, ti, 0))],
        out_specs=pl.BlockSpec((1, tm, d), lambda bi, ti: (bi, ti, 0)),
        out_shape=jax.ShapeDtypeStruct((b, s, d), _F32),
        scratch_shapes=[pltpu.VMEM((POOL_HIST + tm, POOL_WIDTH), _F32),
                        pltpu.VMEM((POOL_HIST, POOL_WIDTH), _F32)],
        compiler_params=pltpu.CompilerParams(dimension_semantics=("arbitrary", "arbitrary"),
                                             vmem_limit_bytes=VMEM_LIMIT_BYTES),
        name="pool_merge_out",
    )(x, mod, g_norm, g_final, w_rest, w_grp, pool_scale, w_pool_up, w_attn_up, w_out, att)


def _rope_tables(s):
    pos = jnp.arange(s, dtype=_F32)
    inv_freq = ROPE_THETA ** (-jnp.arange(0, ROT_DIM, 2, dtype=_F32) / ROT_DIM)
    ang = pos[:, None] * inv_freq[None, :]
    cos, sin = jnp.cos(ang), jnp.sin(ang)
    pad_one = jnp.ones((s, HEAD_DIM - ROT_DIM), _F32)
    pad_zero = jnp.zeros((s, HEAD_DIM - ROT_DIM), _F32)
    cos_l = jnp.concatenate([cos, cos, pad_one] * 2, axis=1)
    sin_l = jnp.concatenate([-sin, sin, pad_zero] * 2, axis=1)
    return cos.T, sin.T, cos_l, sin_l


def kernel(x, c, w_ada, b_ada, g_norm, w_in, w_pool_grp, pool_scale, w_pool_up, w_attn_up, w_out, g_final):
    b, s, d = x.shape
    assert w_ada.shape[0] == 1, "single-layer configuration only"
    assert d == D_MODEL and s % TM_QKV == 0 and s % TM_OUT == 0
    w = w_in[0]
    o_q, o_k, o_v, o_za, o_mp = 2 * POOL_WIDTH, 2 * POOL_WIDTH + ATTN_WIDTH, 2 * POOL_WIDTH + 2 * ATTN_WIDTH, \
        2 * POOL_WIDTH + 3 * ATTN_WIDTH, 2 * POOL_WIDTH + 4 * ATTN_WIDTH
    wqv_t = jnp.concatenate([w[:, o_q:o_k], w[:, o_v:o_za]], axis=1).T.astype(_BF16)
    wk = w[:, o_k:o_v].astype(_BF16)
    w_rest = jnp.concatenate([w[:, :o_q], w[:, o_za:]], axis=1).astype(_BF16)

    mod = _adaln_mod(c, w_ada[0], b_ada[0])
    cos_t, sin_t, cos_l, sin_l = _rope_tables(s)
    g1 = g_norm[0].reshape(1, d)
    q_t, k, v_t, kmean = _qkv_proj(x, mod, g1, wqv_t, wk, cos_t, sin_t, cos_l, sin_l)
    att = _moba_attention(q_t, k, v_t, kmean.reshape(b, s // MOBA_BLOCK, ATTN_WIDTH))
    return _pool_merge_out(x, mod, g1, g_final.reshape(1, d), w_rest,
                           w_pool_grp[0].astype(_BF16), pool_scale[0].reshape(1, POOL_WIDTH),
                           w_pool_up[0].astype(_BF16), w_attn_up[0].astype(_BF16), w_out[0].astype(_BF16), att)
```

```python
import functools

import jax
import jax.numpy as jnp
from jax import lax
from jax.experimental import pallas as pl
from jax.experimental.pallas import tpu as pltpu

D_MODEL = 1024
POOL_WIDTH = 512
POOL_WINDOWS = (2, 4, 8, 16)
POOL_GC = 128
POOL_HIST = 16
HEAD_DIM = 64
N_HEADS = 8
ATTN_WIDTH = N_HEADS * HEAD_DIM
MOBA_BLOCK = 256
MOBA_TOPK = 3
ROT_DIM = 16
ROT_HALF = ROT_DIM // 2
ROPE_THETA = 500000.0
EPS = 1e-6

TM_QKV = 1024
TM_OUT = 512
VMEM_LIMIT_BYTES = 56 * 1024 * 1024

_Q_COL, _K_COL, _V_COL, _ZA_COL = 1024, 1536, 2048, 2560
IN_WIDTH = 5120

_BF16 = jnp.bfloat16
_F32 = jnp.float32
_NT_DIMS = (((1,), (1,)), ((), ()))


def _adaln_norm(x, g, mod_ref):
    ms = jnp.mean(x * x, axis=-1, keepdims=True)
    shift = mod_ref[0, 0:1, :]
    scl = mod_ref[0, 1:2, :]
    return x * lax.rsqrt(ms + EPS) * (g * (1.0 + scl)) + shift


def _mod_kernel(c_ref, w_ref, b_ref, o_ref):
    o_ref[...] = jnp.dot(c_ref[...], w_ref[...], preferred_element_type=_F32,
                         precision=lax.Precision.HIGHEST) + b_ref[...]


def _adaln_mod(c, w_ada, b_ada):
    nb_, d = c.shape
    rows = -(-nb_ // 8) * 8
    c_pad = jnp.zeros((rows, d), _F32).at[:nb_].set(c)
    n_out = w_ada.shape[1]
    out = pl.pallas_call(
        _mod_kernel,
        grid=(n_out // d,),
        in_specs=[pl.BlockSpec((rows, d), lambda j: (0, 0)),
                  pl.BlockSpec((d, d), lambda j: (0, j)),
                  pl.BlockSpec((1, d), lambda j: (0, j))],
        out_specs=pl.BlockSpec((rows, d), lambda j: (0, j)),
        out_shape=jax.ShapeDtypeStruct((rows, n_out), _F32),
        name="adaln_mod",
    )(c_pad, w_ada, b_ada.reshape(1, n_out))
    return out[:nb_].reshape(nb_, n_out // d, d)


def _qkv_kernel(x_ref, mod_ref, g_ref, wq_ref, wv_ref, wk_ref, cos_t_ref, sin_t_ref, cos_l_ref, sin_l_ref,
                qt_ref, k_ref, vt_ref, km_ref, h_ref):
    blocks = range(x_ref.shape[1] // MOBA_BLOCK)
    rows = [slice(i * MOBA_BLOCK, (i + 1) * MOBA_BLOCK) for i in blocks]
    h = [_adaln_norm(x_ref[0, rows[i], :], g_ref[...], mod_ref).astype(_BF16) for i in blocks]

    q_t, v_t, k = [], [], []
    for i in blocks:
        q_t.append(lax.dot_general(wq_ref[...], h[i], _NT_DIMS, preferred_element_type=_F32))
        k.append(jnp.dot(h[i], wk_ref[...], preferred_element_type=_F32))
        v_t.append(lax.dot_general(wv_ref[...], h[i], _NT_DIMS, preferred_element_type=_F32))

    dim = lax.broadcasted_iota(jnp.int32, (MOBA_BLOCK, ATTN_WIDTH), 1) % HEAD_DIM
    for i in blocks:
        h_ref[0, rows[i], :] = h[i]
        cos_t = cos_t_ref[:, rows[i]]
        sin_t = sin_t_ref[:, rows[i]]
        pieces = []
        for hd in range(N_HEADS):
            base = hd * HEAD_DIM
            x1 = q_t[i][base:base + ROT_HALF]
            x2 = q_t[i][base + ROT_HALF:base + ROT_DIM]
            pieces += [x1 * cos_t - x2 * sin_t, x2 * cos_t + x1 * sin_t,
                       q_t[i][base + ROT_DIM:base + HEAD_DIM]]
        qt_ref[0, i] = (jnp.concatenate(pieces, axis=0) * (HEAD_DIM ** -0.5 * LOG2_E)).astype(_BF16)
        vt_ref[0, i] = v_t[i].astype(_BF16)

        cos_l = jnp.concatenate([cos_l_ref[rows[i], :]] * (ATTN_WIDTH // 128), axis=1)
        sin_l = jnp.concatenate([sin_l_ref[rows[i], :]] * (ATTN_WIDTH // 128), axis=1)
        partner = jnp.where(dim < ROT_HALF,
                            pltpu.roll(k[i], ATTN_WIDTH - ROT_HALF, axis=1),
                            pltpu.roll(k[i], ROT_HALF, axis=1))
        kb = jnp.where(dim < ROT_DIM, k[i] * cos_l + partner * sin_l, k[i])
        km_ref[0, i] = jnp.mean(kb, axis=0, keepdims=True)
        k_ref[0, i] = kb.astype(_BF16)


def _qkv_proj(x, mod, g_norm, wq_t, wv_t, w_all, cos_t, sin_t, cos_l, sin_l):
    b, s, d = x.shape
    nb = s // MOBA_BLOCK
    tm = TM_QKV
    nbt = tm // MOBA_BLOCK
    full = lambda *shape: pl.BlockSpec(shape, lambda bi, ti: (0,) * len(shape))
    return pl.pallas_call(
        _qkv_kernel,
        grid=(b, s // tm),
        in_specs=[pl.BlockSpec((1, tm, d), lambda bi, ti: (bi, ti, 0)),
                  pl.BlockSpec((1, 3, d), lambda bi, ti: (bi, 0, 0)),
                  full(1, d),
                  full(ATTN_WIDTH, d),
                  full(ATTN_WIDTH, d),
                  pl.BlockSpec((d, ATTN_WIDTH), lambda bi, ti: (0, _K_COL // ATTN_WIDTH)),
                  pl.BlockSpec((ROT_HALF, tm), lambda bi, ti: (0, ti)),
                  pl.BlockSpec((ROT_HALF, tm), lambda bi, ti: (0, ti)),
                  pl.BlockSpec((tm, 128), lambda bi, ti: (ti, 0)),
                  pl.BlockSpec((tm, 128), lambda bi, ti: (ti, 0))],
        out_specs=[pl.BlockSpec((1, nbt, ATTN_WIDTH, MOBA_BLOCK), lambda bi, ti: (bi, ti, 0, 0)),
                   pl.BlockSpec((1, nbt, MOBA_BLOCK, ATTN_WIDTH), lambda bi, ti: (bi, ti, 0, 0)),
                   pl.BlockSpec((1, nbt, ATTN_WIDTH, MOBA_BLOCK), lambda bi, ti: (bi, ti, 0, 0)),
                   pl.BlockSpec((1, nbt, 1, ATTN_WIDTH), lambda bi, ti: (bi, ti, 0, 0)),
                   pl.BlockSpec((1, tm, d), lambda bi, ti: (bi, ti, 0))],
        out_shape=[jax.ShapeDtypeStruct((b, nb, ATTN_WIDTH, MOBA_BLOCK), _BF16),
                   jax.ShapeDtypeStruct((b, nb, MOBA_BLOCK, ATTN_WIDTH), _BF16),
                   jax.ShapeDtypeStruct((b, nb, ATTN_WIDTH, MOBA_BLOCK), _BF16),
                   jax.ShapeDtypeStruct((b, nb, 1, ATTN_WIDTH), _F32),
                   jax.ShapeDtypeStruct((b, s, d), _BF16)],
        compiler_params=pltpu.CompilerParams(dimension_semantics=("arbitrary", "arbitrary"),
                                             vmem_limit_bytes=VMEM_LIMIT_BYTES),
        name="qkv_proj",
    )(x, mod, g_norm, wq_t, wv_t, w_all, cos_t, sin_t, cos_l, sin_l)


V_EXT = HEAD_DIM + 16
QK_AHEAD = 4
PV_BEHIND = 1
KV_LONG = 4
KV_SHORT = 2
LOG2_E = 1.4426950408889634


def _attn_kernel(qt_ref, k_ref, vt_ref, km_ref, o_ref, qh_ref, sel_ref, acc_ref, s_ref):
    i = pl.program_id(1)
    nb = k_ref.shape[1]
    blk = MOBA_BLOCK
    neg_inf = -jnp.inf

    q_t = qt_ref[0, 0]
    km = km_ref[0].astype(_BF16)
    k_own = k_ref[0, i]
    v_own = vt_ref[0, i]
    pair_row = lax.broadcasted_iota(jnp.int32, (2 * HEAD_DIM, blk), 0)
    blk_id = lax.broadcasted_iota(jnp.int32, (nb, blk), 0)
    key_pos = lax.broadcasted_iota(jnp.int32, (blk, blk), 0)
    qry_pos = lax.broadcasted_iota(jnp.int32, (blk, blk), 1)
    ones = jnp.ones((V_EXT - HEAD_DIM, blk), _BF16)

    def pair_cols(arr, hd):
        p0 = (hd // 2) * 2 * HEAD_DIM
        return arr[:, p0:p0 + 2 * HEAD_DIM]

    def v_ext(v_blk, hd):
        return jnp.concatenate([v_blk[hd * HEAD_DIM:(hd + 1) * HEAD_DIM], ones], axis=0)

    def software_pipeline(items, score, softmax, accumulate):
        n_items = len(items)
        scores, probs = {}, {}
        for step in range(-QK_AHEAD, n_items + PV_BEHIND):
            if 0 <= step + QK_AHEAD < n_items:
                scores[step + QK_AHEAD] = score(items[step + QK_AHEAD])
            if 0 <= step < n_items:
                probs[step] = softmax(items[step], scores.pop(step))
            if 0 <= step - PV_BEHIND < n_items:
                accumulate(items[step - PV_BEHIND], probs.pop(step - PV_BEHIND))

    heads = list(range(N_HEADS))
    for hd in heads:
        p0 = (hd // 2) * 2 * HEAD_DIM
        in_head = (pair_row >= HEAD_DIM) == (hd % 2 == 1)
        qh_ref[hd] = jnp.where(in_head, q_t[p0:p0 + 2 * HEAD_DIM], jnp.zeros((), _BF16))

    def gate_scores(hd):
        return jnp.dot(pair_cols(km, hd), qh_ref[hd], preferred_element_type=_F32)

    def select_blocks(hd, gate):
        gate = jnp.where(blk_id < i, gate, neg_inf)
        sel = jnp.zeros((nb, blk), _F32)
        for _ in range(MOBA_TOPK):
            best = jnp.max(gate, axis=0, keepdims=True)
            is_best = (gate == best) & (gate > neg_inf)
            idx = jnp.min(jnp.where(is_best, blk_id, nb), axis=0, keepdims=True)
            pick = blk_id == idx
            sel = jnp.where(pick, 1.0, sel)
            gate = jnp.where(pick, neg_inf, gate)
        sel_ref[hd] = sel

    software_pipeline(heads, gate_scores, select_blocks, lambda hd, _: None)

    m_own = [None] * N_HEADS

    def qk(k_blk, hd):
        return jnp.dot(pair_cols(k_blk, hd), qh_ref[hd], preferred_element_type=_F32)

    handover = [("next", n) for n in range(QK_AHEAD)]

    def tile_of(n):
        return divmod(n, N_HEADS)

    def own_score(item):
        kind, n = item
        return qk(k_own, n) if kind == "own" else qk(k_ref[0, 0], n)

    def own_softmax(item, s):
        kind, n = item
        if kind == "next":
            s_ref[n] = s
            return None
        s = jnp.where(key_pos <= qry_pos, s, neg_inf)
        m_own[n] = jnp.max(s, axis=0, keepdims=True)
        return jnp.exp2(s - m_own[n]).astype(_BF16)

    def own_accumulate(item, p):
        kind, hd = item
        if kind == "own":
            acc_ref[hd] = jnp.dot(v_ext(v_own, hd), p, preferred_element_type=_F32)

    software_pipeline([("own", hd) for hd in heads] + handover, own_score, own_softmax, own_accumulate)

    def past_blocks(n_blocks, first, t, m_run):
        j0 = first + t * n_blocks
        j_next = jnp.minimum(j0 + n_blocks, nb - 1)
        k_j = [k_ref[0, j0 + u] for u in range(n_blocks)]
        v_j = [vt_ref[0, j0 + u] for u in range(n_blocks)]
        m_cur = list(m_run)

        def score(item):
            kind, n = item
            u, hd = tile_of(n)
            if kind == "next":
                return qk(k_ref[0, j_next], n)
            return s_ref[n] if n < QK_AHEAD else qk(k_j[u], hd)

        def softmax(item, s):
            kind, n = item
            if kind == "next":
                s_ref[n] = s
                return None
            u, hd = tile_of(n)
            chosen = sel_ref[hd, pl.ds(j0 + u, 1), :] > 0.5
            m_old = m_cur[hd]
            m_new = jnp.where(chosen, jnp.maximum(m_old, jnp.max(s, axis=0, keepdims=True)), m_old)
            m_cur[hd] = m_new
            shift = jnp.where(chosen, m_new, jnp.inf)
            return jnp.exp2(s - shift).astype(_BF16), jnp.exp2(m_old - m_new)

        def accumulate(item, p_alpha):
            kind, n = item
            if kind == "tile":
                u, hd = tile_of(n)
                p, alpha = p_alpha
                acc_ref[hd] = alpha * acc_ref[hd] + jnp.dot(v_ext(v_j[u], hd), p, preferred_element_type=_F32)

        software_pipeline([("tile", n) for n in range(n_blocks * N_HEADS)] + handover,
                          score, softmax, accumulate)
        return tuple(m_cur)

    n_long = i // KV_LONG
    n_short = (i - n_long * KV_LONG + KV_SHORT - 1) // KV_SHORT
    m_run = lax.fori_loop(0, n_long, functools.partial(past_blocks, KV_LONG, 0), tuple(m_own))
    lax.fori_loop(0, n_short, functools.partial(past_blocks, KV_SHORT, n_long * KV_LONG), m_run)

    outs = []
    for hd in range(N_HEADS):
        acc = acc_ref[hd]
        outs.append(acc[:HEAD_DIM] / acc[HEAD_DIM:HEAD_DIM + 1])
    o_ref[0] = jnp.concatenate(outs, axis=0).T.astype(o_ref.dtype)


def _moba_attention(q_t, k, v_t, kmean):
    b, nb = k.shape[0], k.shape[1]
    s = nb * MOBA_BLOCK
    assert nb % KV_SHORT == 0 and QK_AHEAD <= N_HEADS
    return pl.pallas_call(
        _attn_kernel,
        grid=(b, nb),
        in_specs=[pl.BlockSpec((1, 1, ATTN_WIDTH, MOBA_BLOCK), lambda bi, qi: (bi, qi, 0, 0)),
                  pl.BlockSpec((1, nb, MOBA_BLOCK, ATTN_WIDTH), lambda bi, qi: (bi, 0, 0, 0)),
                  pl.BlockSpec((1, nb, ATTN_WIDTH, MOBA_BLOCK), lambda bi, qi: (bi, 0, 0, 0)),
                  pl.BlockSpec((1, nb, ATTN_WIDTH), lambda bi, qi: (bi, 0, 0))],
        out_specs=pl.BlockSpec((1, MOBA_BLOCK, ATTN_WIDTH), lambda bi, qi: (bi, qi, 0)),
        out_shape=jax.ShapeDtypeStruct((b, s, ATTN_WIDTH), _BF16),
        scratch_shapes=[pltpu.VMEM((N_HEADS, 2 * HEAD_DIM, MOBA_BLOCK), _BF16),
                        pltpu.VMEM((N_HEADS, nb, MOBA_BLOCK), _F32),
                        pltpu.VMEM((N_HEADS, V_EXT, MOBA_BLOCK), _F32),
                        pltpu.VMEM((QK_AHEAD, MOBA_BLOCK, MOBA_BLOCK), _F32)],
        compiler_params=pltpu.CompilerParams(dimension_semantics=("arbitrary", "arbitrary"),
                                             vmem_limit_bytes=VMEM_LIMIT_BYTES),
        name="moba_attention",
    )(q_t, k, v_t, kmean)


_HEAD_W = _Q_COL
_TAIL_W = IN_WIDTH - _ZA_COL
SUB_ROWS = 256


def _out_kernel(x_ref, h_ref, mod_ref, gf_ref, wh_ref, wt_ref, wg_ref, ps_ref, wpu_ref, wau_ref, wo_ref, att_ref,
                out_ref, hist_ref, carry_ref):
    t = pl.program_id(1)
    tm = x_ref.shape[1]
    subs = range(tm // SUB_ROWS)
    rows = [slice(s * SUB_ROWS, (s + 1) * SUB_ROWS) for s in subs]

    def mm(a, w):
        return jnp.dot(a.astype(_BF16), w, preferred_element_type=_F32)

    @pl.when(t == 0)
    def _():
        carry_ref[...] = jnp.zeros_like(carry_ref)

    h = [h_ref[0, r, :] for r in rows]
    u = [mm(h[s], wh_ref[:, :POOL_WIDTH]) for s in subs]
    z_attn = [mm(h[s], wt_ref[:, :ATTN_WIDTH]) for s in subs]
    z_pool = [mm(h[s], wh_ref[:, POOL_WIDTH:]) for s in subs]
    m_attn = [mm(h[s], wt_ref[:, ATTN_WIDTH + D_MODEL:]) for s in subs]
    m_pool = [mm(h[s], wt_ref[:, ATTN_WIDTH:ATTN_WIDTH + D_MODEL]) for s in subs]

    hist_ref[0:POOL_HIST, :] = carry_ref[...]
    for s in subs:
        hist_ref[POOL_HIST + s * SUB_ROWS:POOL_HIST + (s + 1) * SUB_ROWS, :] = u[s]
    carry_ref[...] = u[-1][SUB_ROWS - POOL_HIST:, :]

    y_attn = [mm(att_ref[0, rows[s], :].astype(_F32) * jax.nn.silu(z_attn[s]), wau_ref[...]) for s in subs]

    y_grp = []
    for s in subs:
        r0 = POOL_HIST + s * SUB_ROWS
        pos = t * tm + s * SUB_ROWS + lax.broadcasted_iota(jnp.int32, (SUB_ROWS, POOL_GC), 0)
        ys = []
        for g, w in enumerate(POOL_WINDOWS):
            lanes = slice(g * POOL_GC, (g + 1) * POOL_GC)
            acc = hist_ref[r0:r0 + SUB_ROWS, lanes]
            for back in range(1, w):
                acc = acc + hist_ref[r0 - back:r0 - back + SUB_ROWS, lanes]
            cnt = jnp.minimum(pos + 1, w).astype(_F32)
            ys.append(mm(acc / cnt - u[s][:, lanes], wg_ref[g]))
        y_grp.append(jnp.concatenate(ys, axis=1))
    y_pool = [mm(y_grp[s] * ps_ref[...] * jax.nn.silu(z_pool[s]), wpu_ref[...]) for s in subs]

    res = [mm(jax.nn.sigmoid(m_pool[s]) * y_pool[s] + jax.nn.sigmoid(m_attn[s]) * y_attn[s], wo_ref[...])
           for s in subs]
    for s in subs:
        xn = x_ref[0, rows[s], :] + mod_ref[0, 2:3, :] * res[s]
        ms = jnp.mean(xn * xn, axis=-1, keepdims=True)
        out_ref[0, rows[s], :] = xn * lax.rsqrt(ms + EPS) * gf_ref[...]


def _pool_merge_out(x, h, mod, g_final, w_all, w_grp, pool_scale, w_pool_up, w_attn_up, w_out, att):
    assert _ZA_COL % _TAIL_W == 0
    b, s, d = x.shape
    tm = TM_OUT
    full = lambda *shape: pl.BlockSpec(shape, lambda bi, ti: (0,) * len(shape))
    return pl.pallas_call(
        _out_kernel,
        grid=(b, s // tm),
        in_specs=[pl.BlockSpec((1, tm, d), lambda bi, ti: (bi, ti, 0)),
                  pl.BlockSpec((1, tm, d), lambda bi, ti: (bi, ti, 0)),
                  pl.BlockSpec((1, 3, d), lambda bi, ti: (bi, 0, 0)),
                  full(1, d),
                  pl.BlockSpec((d, _HEAD_W), lambda bi, ti: (0, 0)),
                  pl.BlockSpec((d, _TAIL_W), lambda bi, ti: (0, _ZA_COL // _TAIL_W)),
                  full(len(POOL_WINDOWS), POOL_GC, POOL_GC),
                  full(1, POOL_WIDTH),
                  full(POOL_WIDTH, d), full(ATTN_WIDTH, d), full(d, d),
                  pl.BlockSpec((1, tm, ATTN_WIDTH), lambda bi, ti: (bi, ti, 0))],
        out_specs=pl.BlockSpec((1, tm, d), lambda bi, ti: (bi, ti, 0)),
        out_shape=jax.ShapeDtypeStruct((b, s, d), _F32),
        scratch_shapes=[pltpu.VMEM((POOL_HIST + tm, POOL_WIDTH), _F32),
                        pltpu.VMEM((POOL_HIST, POOL_WIDTH), _F32)],
        compiler_params=pltpu.CompilerParams(dimension_semantics=("arbitrary", "arbitrary"),
                                             vmem_limit_bytes=VMEM_LIMIT_BYTES),
        name="pool_merge_out",
    )(x, h, mod, g_final, w_all, w_all, w_grp, pool_scale, w_pool_up, w_attn_up, w_out, att)


def _rope_tables(s):
    pos = jnp.arange(s, dtype=_F32)
    inv_freq = ROPE_THETA ** (-jnp.arange(0, ROT_DIM, 2, dtype=_F32) / ROT_DIM)
    ang = pos[:, None] * inv_freq[None, :]
    cos, sin = jnp.cos(ang), jnp.sin(ang)
    pad_one = jnp.ones((s, HEAD_DIM - ROT_DIM), _F32)
    pad_zero = jnp.zeros((s, HEAD_DIM - ROT_DIM), _F32)
    cos_l = jnp.concatenate([cos, cos, pad_one] * 2, axis=1)
    sin_l = jnp.concatenate([-sin, sin, pad_zero] * 2, axis=1)
    return cos.T, sin.T, cos_l, sin_l


def kernel(x, c, w_ada, b_ada, g_norm, w_in, w_pool_grp, pool_scale, w_pool_up, w_attn_up, w_out, g_final):
    b, s, d = x.shape
    assert w_ada.shape[0] == 1, "single-layer configuration only"
    assert d == D_MODEL and s % TM_QKV == 0 and s % TM_OUT == 0
    assert w_in.shape[1:] == (d, IN_WIDTH)
    w_all = w_in[0].astype(_BF16)
    wq_t = w_all[:, _Q_COL:_K_COL].T
    wv_t = w_all[:, _V_COL:_ZA_COL].T

    mod = _adaln_mod(c, w_ada[0], b_ada[0])
    cos_t, sin_t, cos_l, sin_l = _rope_tables(s)
    g1 = g_norm[0].reshape(1, d)
    q_t, k, v_t, kmean, h = _qkv_proj(x, mod, g1, wq_t, wv_t, w_all, cos_t, sin_t, cos_l, sin_l)
    att = _moba_attention(q_t, k, v_t, kmean.reshape(b, s // MOBA_BLOCK, ATTN_WIDTH))
    return _pool_merge_out(x, h, mod, g_final.reshape(1, d), w_all,
                           w_pool_grp[0].astype(_BF16), pool_scale[0].reshape(1, POOL_WIDTH),
                           w_pool_up[0].astype(_BF16), w_attn_up[0].astype(_BF16), w_out[0].astype(_BF16), att)
```

```python
import functools

import jax
import jax.numpy as jnp
from jax import lax
from jax.experimental import pallas as pl
from jax.experimental.pallas import tpu as pltpu

D_MODEL = 1024
POOL_WIDTH = 512
POOL_WINDOWS = (2, 4, 8, 16)
POOL_GC = 128
POOL_HIST = 16
HEAD_DIM = 64
N_HEADS = 8
ATTN_WIDTH = N_HEADS * HEAD_DIM
MOBA_BLOCK = 256
MOBA_TOPK = 3
ROT_DIM = 16
ROT_HALF = ROT_DIM // 2
ROPE_THETA = 500000.0
EPS = 1e-6

TM_QKV = 1024
TM_OUT = 512
VMEM_LIMIT_BYTES = 56 * 1024 * 1024

_Q_COL, _K_COL, _V_COL, _ZA_COL = 1024, 1536, 2048, 2560
IN_WIDTH = 5120

_BF16 = jnp.bfloat16
_F32 = jnp.float32
_NT_DIMS = (((1,), (1,)), ((), ()))


def _adaln_norm(x, g, mod_ref):
    ms = jnp.mean(x * x, axis=-1, keepdims=True)
    shift = mod_ref[0, 0:1, :]
    scl = mod_ref[0, 1:2, :]
    return x * lax.rsqrt(ms + EPS) * (g * (1.0 + scl)) + shift


def _mod_kernel(c_ref, w_ref, b_ref, o_ref):
    o_ref[...] = jnp.dot(c_ref[...], w_ref[...], preferred_element_type=_F32,
                         precision=lax.Precision.HIGHEST) + b_ref[...]


def _adaln_mod(c, w_ada, b_ada):
    nb_, d = c.shape
    rows = -(-nb_ // 8) * 8
    c_pad = jnp.zeros((rows, d), _F32).at[:nb_].set(c)
    n_out = w_ada.shape[1]
    out = pl.pallas_call(
        _mod_kernel,
        grid=(n_out // d,),
        in_specs=[pl.BlockSpec((rows, d), lambda j: (0, 0)),
                  pl.BlockSpec((d, d), lambda j: (0, j)),
                  pl.BlockSpec((1, d), lambda j: (0, j))],
        out_specs=pl.BlockSpec((rows, d), lambda j: (0, j)),
        out_shape=jax.ShapeDtypeStruct((rows, n_out), _F32),
        name="adaln_mod",
    )(c_pad, w_ada, b_ada.reshape(1, n_out))
    return out[:nb_].reshape(nb_, n_out // d, d)


def _qkv_kernel(x_ref, mod_ref, g_ref, wq_ref, wv_ref, wk_ref, cos_t_ref, sin_t_ref, cos_l_ref, sin_l_ref,
                qt_ref, k_ref, vt_ref, km_ref, h_ref):
    blocks = range(x_ref.shape[1] // MOBA_BLOCK)
    rows = [slice(i * MOBA_BLOCK, (i + 1) * MOBA_BLOCK) for i in blocks]
    h = [_adaln_norm(x_ref[0, rows[i], :], g_ref[...], mod_ref).astype(_BF16) for i in blocks]

    q_t, v_t, k = [], [], []
    for i in blocks:
        q_t.append(lax.dot_general(wq_ref[...], h[i], _NT_DIMS, preferred_element_type=_F32))
        k.append(jnp.dot(h[i], wk_ref[...], preferred_element_type=_F32))
        v_t.append(lax.dot_general(wv_ref[...], h[i], _NT_DIMS, preferred_element_type=_F32))

    dim = lax.broadcasted_iota(jnp.int32, (MOBA_BLOCK, ATTN_WIDTH), 1) % HEAD_DIM
    for i in blocks:
        h_ref[0, rows[i], :] = h[i]
        cos_t = cos_t_ref[:, rows[i]]
        sin_t = sin_t_ref[:, rows[i]]
        pieces = []
        for hd in range(N_HEADS):
            base = hd * HEAD_DIM
            x1 = q_t[i][base:base + ROT_HALF]
            x2 = q_t[i][base + ROT_HALF:base + ROT_DIM]
            pieces += [x1 * cos_t - x2 * sin_t, x2 * cos_t + x1 * sin_t,
                       q_t[i][base + ROT_DIM:base + HEAD_DIM]]
        qt_ref[0, i] = (jnp.concatenate(pieces, axis=0) * (HEAD_DIM ** -0.5 * LOG2_E)).astype(_BF16)
        vt_ref[0, i] = v_t[i].astype(_BF16)

        cos_l = jnp.concatenate([cos_l_ref[rows[i], :]] * (ATTN_WIDTH // 128), axis=1)
        sin_l = jnp.concatenate([sin_l_ref[rows[i], :]] * (ATTN_WIDTH // 128), axis=1)
        partner = jnp.where(dim < ROT_HALF,
                            pltpu.roll(k[i], ATTN_WIDTH - ROT_HALF, axis=1),
                            pltpu.roll(k[i], ROT_HALF, axis=1))
        kb = jnp.where(dim < ROT_DIM, k[i] * cos_l + partner * sin_l, k[i])
        km_ref[0, i] = jnp.mean(kb, axis=0, keepdims=True)
        k_ref[0, i] = kb.astype(_BF16)


def _qkv_proj(x, mod, g_norm, wq_t, wv_t, w_all, cos_t, sin_t, cos_l, sin_l):
    b, s, d = x.shape
    nb = s // MOBA_BLOCK
    tm = TM_QKV
    nbt = tm // MOBA_BLOCK
    full = lambda *shape: pl.BlockSpec(shape, lambda bi, ti: (0,) * len(shape))
    return pl.pallas_call(
        _qkv_kernel,
        grid=(b, s // tm),
        in_specs=[pl.BlockSpec((1, tm, d), lambda bi, ti: (bi, ti, 0)),
                  pl.BlockSpec((1, 3, d), lambda bi, ti: (bi, 0, 0)),
                  full(1, d),
                  full(ATTN_WIDTH, d),
                  full(ATTN_WIDTH, d),
                  pl.BlockSpec((d, ATTN_WIDTH), lambda bi, ti: (0, _K_COL // ATTN_WIDTH)),
                  pl.BlockSpec((ROT_HALF, tm), lambda bi, ti: (0, ti)),
                  pl.BlockSpec((ROT_HALF, tm), lambda bi, ti: (0, ti)),
                  pl.BlockSpec((tm, 128), lambda bi, ti: (ti, 0)),
                  pl.BlockSpec((tm, 128), lambda bi, ti: (ti, 0))],
        out_specs=[pl.BlockSpec((1, nbt, ATTN_WIDTH, MOBA_BLOCK), lambda bi, ti: (bi, ti, 0, 0)),
                   pl.BlockSpec((1, nbt, MOBA_BLOCK, ATTN_WIDTH), lambda bi, ti: (bi, ti, 0, 0)),
                   pl.BlockSpec((1, nbt, ATTN_WIDTH, MOBA_BLOCK), lambda bi, ti: (bi, ti, 0, 0)),
                   pl.BlockSpec((1, nbt, 1, ATTN_WIDTH), lambda bi, ti: (bi, ti, 0, 0)),
                   pl.BlockSpec((1, tm, d), lambda bi, ti: (bi, ti, 0))],
        out_shape=[jax.ShapeDtypeStruct((b, nb, ATTN_WIDTH, MOBA_BLOCK), _BF16),
                   jax.ShapeDtypeStruct((b, nb, MOBA_BLOCK, ATTN_WIDTH), _BF16),
                   jax.ShapeDtypeStruct((b, nb, ATTN_WIDTH, MOBA_BLOCK), _BF16),
                   jax.ShapeDtypeStruct((b, nb, 1, ATTN_WIDTH), _F32),
                   jax.ShapeDtypeStruct((b, s, d), _BF16)],
        compiler_params=pltpu.CompilerParams(dimension_semantics=("arbitrary", "arbitrary"),
                                             vmem_limit_bytes=VMEM_LIMIT_BYTES),
        name="qkv_proj",
    )(x, mod, g_norm, wq_t, wv_t, w_all, cos_t, sin_t, cos_l, sin_l)


V_EXT = HEAD_DIM + 16
QK_AHEAD = 4
PV_BEHIND = 1
KV_LONG = 8
KV_SHORT = 2
LOG2_E = 1.4426950408889634


def _attn_kernel(qt_ref, k_ref, vt_ref, km_ref, o_ref, qh_ref, sel_ref, acc_ref, s_ref):
    i = pl.program_id(1)
    nb = k_ref.shape[1]
    blk = MOBA_BLOCK
    neg_inf = -jnp.inf

    q_t = qt_ref[0, 0]
    km = km_ref[0].astype(_BF16)
    k_own = k_ref[0, i]
    v_own = vt_ref[0, i]
    pair_row = lax.broadcasted_iota(jnp.int32, (2 * HEAD_DIM, blk), 0)
    blk_id = lax.broadcasted_iota(jnp.int32, (nb, blk), 0)
    key_pos = lax.broadcasted_iota(jnp.int32, (blk, blk), 0)
    qry_pos = lax.broadcasted_iota(jnp.int32, (blk, blk), 1)
    ones = jnp.ones((V_EXT - HEAD_DIM, blk), _BF16)

    def pair_cols(arr, hd):
        p0 = (hd // 2) * 2 * HEAD_DIM
        return arr[:, p0:p0 + 2 * HEAD_DIM]

    def v_ext(v_blk, hd):
        return jnp.concatenate([v_blk[hd * HEAD_DIM:(hd + 1) * HEAD_DIM], ones], axis=0)

    def software_pipeline(items, score, softmax, accumulate):
        n_items = len(items)
        scores, probs = {}, {}
        for step in range(-QK_AHEAD, n_items + PV_BEHIND):
            if 0 <= step + QK_AHEAD < n_items:
                scores[step + QK_AHEAD] = score(items[step + QK_AHEAD])
            if 0 <= step < n_items:
                probs[step] = softmax(items[step], scores.pop(step))
            if 0 <= step - PV_BEHIND < n_items:
                accumulate(items[step - PV_BEHIND], probs.pop(step - PV_BEHIND))

    heads = list(range(N_HEADS))
    for hd in heads:
        p0 = (hd // 2) * 2 * HEAD_DIM
        in_head = (pair_row >= HEAD_DIM) == (hd % 2 == 1)
        qh_ref[hd] = jnp.where(in_head, q_t[p0:p0 + 2 * HEAD_DIM], jnp.zeros((), _BF16))

    def gate_scores(hd):
        return jnp.dot(pair_cols(km, hd), qh_ref[hd], preferred_element_type=_F32)

    def select_blocks(hd, gate):
        past = blk_id < i
        gate = jnp.where(past, gate, neg_inf)
        sel = jnp.zeros((nb, blk), _F32)
        for _ in range(MOBA_TOPK):
            best = jnp.max(gate, axis=0, keepdims=True)
            idx = jnp.min(jnp.where(gate == best, blk_id, nb), axis=0, keepdims=True)
            pick = blk_id == idx
            sel = jnp.where(pick, 1.0, sel)
            gate = jnp.where(pick, neg_inf, gate)
        sel_ref[hd] = jnp.where(past, sel, 0.0)

    software_pipeline(heads, gate_scores, select_blocks, lambda hd, _: None)

    m_own = [None] * N_HEADS

    def qk(k_blk, hd):
        return jnp.dot(pair_cols(k_blk, hd), qh_ref[hd], preferred_element_type=_F32)

    handover = [("next", n) for n in range(QK_AHEAD)]

    def tile_of(n):
        return divmod(n, N_HEADS)

    def own_score(item):
        kind, n = item
        return qk(k_own, n) if kind == "own" else qk(k_ref[0, 0], n)

    def own_softmax(item, s):
        kind, n = item
        if kind == "next":
            s_ref[n] = s
            return None
        s = jnp.where(key_pos <= qry_pos, s, neg_inf)
        m_own[n] = jnp.max(s, axis=0, keepdims=True)
        return jnp.exp2(s - m_own[n]).astype(_BF16)

    def own_accumulate(item, p):
        kind, hd = item
        if kind == "own":
            acc_ref[hd] = jnp.dot(v_ext(v_own, hd), p, preferred_element_type=_F32)

    software_pipeline([("own", hd) for hd in heads] + handover, own_score, own_softmax, own_accumulate)

    def past_blocks(n_blocks, first, t, m_run):
        j0 = first + t * n_blocks
        j_next = jnp.minimum(j0 + n_blocks, nb - 1)
        k_j = [k_ref[0, j0 + u] for u in range(n_blocks)]
        v_j = [vt_ref[0, j0 + u] for u in range(n_blocks)]
        m_cur = list(m_run)

        def score(item):
            kind, n = item
            u, hd = tile_of(n)
            if kind == "next":
                return qk(k_ref[0, j_next], n)
            return s_ref[n] if n < QK_AHEAD else qk(k_j[u], hd)

        def softmax(item, s):
            kind, n = item
            if kind == "next":
                s_ref[n] = s
                return None
            u, hd = tile_of(n)
            chosen = sel_ref[hd, pl.ds(j0 + u, 1), :] > 0.5
            m_old = m_cur[hd]
            m_new = jnp.where(chosen, jnp.maximum(m_old, jnp.max(s, axis=0, keepdims=True)), m_old)
            m_cur[hd] = m_new
            shift = jnp.where(chosen, m_new, jnp.inf)
            return jnp.exp2(s - shift).astype(_BF16), jnp.exp2(m_old - m_new)

        def accumulate(item, p_alpha):
            kind, n = item
            if kind == "tile":
                u, hd = tile_of(n)
                p, alpha = p_alpha
                acc_ref[hd] = alpha * acc_ref[hd] + jnp.dot(v_ext(v_j[u], hd), p, preferred_element_type=_F32)

        software_pipeline([("tile", n) for n in range(n_blocks * N_HEADS)] + handover,
                          score, softmax, accumulate)
        return tuple(m_cur)

    n_long = i // KV_LONG
    n_short = (i - n_long * KV_LONG + KV_SHORT - 1) // KV_SHORT
    m_run = lax.fori_loop(0, n_long, functools.partial(past_blocks, KV_LONG, 0), tuple(m_own))
    lax.fori_loop(0, n_short, functools.partial(past_blocks, KV_SHORT, n_long * KV_LONG), m_run)

    outs = []
    for hd in range(N_HEADS):
        acc = acc_ref[hd]
        outs.append(acc[:HEAD_DIM] / acc[HEAD_DIM:HEAD_DIM + 1])
    o_ref[0] = jnp.concatenate(outs, axis=0).T.astype(o_ref.dtype)


def _moba_attention(q_t, k, v_t, kmean):
    b, nb = k.shape[0], k.shape[1]
    s = nb * MOBA_BLOCK
    assert nb % KV_SHORT == 0 and QK_AHEAD <= N_HEADS
    return pl.pallas_call(
        _attn_kernel,
        grid=(b, nb),
        in_specs=[pl.BlockSpec((1, 1, ATTN_WIDTH, MOBA_BLOCK), lambda bi, qi: (bi, qi, 0, 0)),
                  pl.BlockSpec((1, nb, MOBA_BLOCK, ATTN_WIDTH), lambda bi, qi: (bi, 0, 0, 0)),
                  pl.BlockSpec((1, nb, ATTN_WIDTH, MOBA_BLOCK), lambda bi, qi: (bi, 0, 0, 0)),
                  pl.BlockSpec((1, nb, ATTN_WIDTH), lambda bi, qi: (bi, 0, 0))],
        out_specs=pl.BlockSpec((1, MOBA_BLOCK, ATTN_WIDTH), lambda bi, qi: (bi, qi, 0)),
        out_shape=jax.ShapeDtypeStruct((b, s, ATTN_WIDTH), _BF16),
        scratch_shapes=[pltpu.VMEM((N_HEADS, 2 * HEAD_DIM, MOBA_BLOCK), _BF16),
                        pltpu.VMEM((N_HEADS, nb, MOBA_BLOCK), _F32),
                        pltpu.VMEM((N_HEADS, V_EXT, MOBA_BLOCK), _F32),
                        pltpu.VMEM((QK_AHEAD, MOBA_BLOCK, MOBA_BLOCK), _F32)],
        compiler_params=pltpu.CompilerParams(dimension_semantics=("arbitrary", "arbitrary"),
                                             vmem_limit_bytes=VMEM_LIMIT_BYTES),
        name="moba_attention",
    )(q_t, k, v_t, kmean)


_HEAD_W = _Q_COL
_TAIL_W = IN_WIDTH - _ZA_COL
SUB_ROWS = 256


def _out_kernel(x_ref, h_ref, mod_ref, gf_ref, wh_ref, wt_ref, wg_ref, ps_ref, wpu_ref, wau_ref, wo_ref, att_ref,
                out_ref, hist_ref, carry_ref):
    t = pl.program_id(1)
    tm = x_ref.shape[1]
    subs = range(tm // SUB_ROWS)
    rows = [slice(s * SUB_ROWS, (s + 1) * SUB_ROWS) for s in subs]

    def mm(a, w):
        return jnp.dot(a.astype(_BF16), w, preferred_element_type=_F32)

    @pl.when(t == 0)
    def _():
        carry_ref[...] = jnp.zeros_like(carry_ref)

    h = [h_ref[0, r, :] for r in rows]
    u = [mm(h[s], wh_ref[:, :POOL_WIDTH]) for s in subs]
    z_attn = [mm(h[s], wt_ref[:, :ATTN_WIDTH]) for s in subs]
    z_pool = [mm(h[s], wh_ref[:, POOL_WIDTH:]) for s in subs]
    m_attn = [mm(h[s], wt_ref[:, ATTN_WIDTH + D_MODEL:]) for s in subs]
    m_pool = [mm(h[s], wt_ref[:, ATTN_WIDTH:ATTN_WIDTH + D_MODEL]) for s in subs]

    hist_ref[0:POOL_HIST, :] = carry_ref[...]
    for s in subs:
        hist_ref[POOL_HIST + s * SUB_ROWS:POOL_HIST + (s + 1) * SUB_ROWS, :] = u[s]
    carry_ref[...] = u[-1][SUB_ROWS - POOL_HIST:, :]

    y_attn = [mm(att_ref[0, rows[s], :].astype(_F32) * jax.nn.silu(z_attn[s]), wau_ref[...]) for s in subs]

    y_grp = []
    for s in subs:
        r0 = POOL_HIST + s * SUB_ROWS
        pos = t * tm + s * SUB_ROWS + lax.broadcasted_iota(jnp.int32, (SUB_ROWS, POOL_GC), 0)
        ys = []
        for g, w in enumerate(POOL_WINDOWS):
            lanes = slice(g * POOL_GC, (g + 1) * POOL_GC)
            acc = hist_ref[r0:r0 + SUB_ROWS, lanes]
            for back in range(1, w):
                acc = acc + hist_ref[r0 - back:r0 - back + SUB_ROWS, lanes]
            cnt = jnp.minimum(pos + 1, w).astype(_F32)
            ys.append(mm(acc / cnt - u[s][:, lanes], wg_ref[g]))
        y_grp.append(jnp.concatenate(ys, axis=1))
    y_pool = [mm(y_grp[s] * ps_ref[...] * jax.nn.silu(z_pool[s]), wpu_ref[...]) for s in subs]

    res = [mm(jax.nn.sigmoid(m_pool[s]) * y_pool[s] + jax.nn.sigmoid(m_attn[s]) * y_attn[s], wo_ref[...])
           for s in subs]
    for s in subs:
        xn = x_ref[0, rows[s], :] + mod_ref[0, 2:3, :] * res[s]
        ms = jnp.mean(xn * xn, axis=-1, keepdims=True)
        out_ref[0, rows[s], :] = xn * lax.rsqrt(ms + EPS) * gf_ref[...]


def _pool_merge_out(x, h, mod, g_final, w_all, w_grp, pool_scale, w_pool_up, w_attn_up, w_out, att):
    assert _ZA_COL % _TAIL_W == 0
    b, s, d = x.shape
    tm = TM_OUT
    full = lambda *shape: pl.BlockSpec(shape, lambda bi, ti: (0,) * len(shape))
    return pl.pallas_call(
        _out_kernel,
        grid=(b, s // tm),
        in_specs=[pl.BlockSpec((1, tm, d), lambda bi, ti: (bi, ti, 0)),
                  pl.BlockSpec((1, tm, d), lambda bi, ti: (bi, ti, 0)),
                  pl.BlockSpec((1, 3, d), lambda bi, ti: (bi, 0, 0)),
                  full(1, d),
                  pl.BlockSpec((d, _HEAD_W), lambda bi, ti: (0, 0)),
                  pl.BlockSpec((d, _TAIL_W), lambda bi, ti: (0, _ZA_COL // _TAIL_W)),
                  full(len(POOL_WINDOWS), POOL_GC, POOL_GC),
                  full(1, POOL_WIDTH),
                  full(POOL_WIDTH, d), full(ATTN_WIDTH, d), full(d, d),
                  pl.BlockSpec((1, tm, ATTN_WIDTH), lambda bi, ti: (bi, ti, 0))],
        out_specs=pl.BlockSpec((1, tm, d), lambda bi, ti: (bi, ti, 0)),
        out_shape=jax.ShapeDtypeStruct((b, s, d), _F32),
        scratch_shapes=[pltpu.VMEM((POOL_HIST + tm, POOL_WIDTH), _F32),
                        pltpu.VMEM((POOL_HIST, POOL_WIDTH), _F32)],
        compiler_params=pltpu.CompilerParams(dimension_semantics=("arbitrary", "arbitrary"),
                                             vmem_limit_bytes=VMEM_LIMIT_BYTES),
        name="pool_merge_out",
    )(x, h, mod, g_final, w_all, w_all, w_grp, pool_scale, w_pool_up, w_attn_up, w_out, att)


def _rope_tables(s):
    pos = jnp.arange(s, dtype=_F32)
    inv_freq = ROPE_THETA ** (-jnp.arange(0, ROT_DIM, 2, dtype=_F32) / ROT_DIM)
    ang_t = inv_freq[:, None] * pos[None, :]
    dim = jnp.arange(128) % HEAD_DIM
    rot = dim < ROT_DIM
    ang_l = pos[:, None] * inv_freq[dim % ROT_HALF][None, :]
    cos_l = jnp.where(rot, jnp.cos(ang_l), 1.0)
    sin_l = jnp.where(rot, jnp.where(dim < ROT_HALF, -1.0, 1.0) * jnp.sin(ang_l), 0.0)
    return jnp.cos(ang_t), jnp.sin(ang_t), cos_l, sin_l


def kernel(x, c, w_ada, b_ada, g_norm, w_in, w_pool_grp, pool_scale, w_pool_up, w_attn_up, w_out, g_final):
    b, s, d = x.shape
    assert w_ada.shape[0] == 1, "single-layer configuration only"
    assert d == D_MODEL and s % TM_QKV == 0 and s % TM_OUT == 0
    assert w_in.shape[1:] == (d, IN_WIDTH)
    w_all = w_in[0].astype(_BF16)
    wq_t = w_all[:, _Q_COL:_K_COL].T
    wv_t = w_all[:, _V_COL:_ZA_COL].T

    mod = _adaln_mod(c, w_ada[0], b_ada[0])
    cos_t, sin_t, cos_l, sin_l = _rope_tables(s)
    g1 = g_norm[0].reshape(1, d)
    q_t, k, v_t, kmean, h = _qkv_proj(x, mod, g1, wq_t, wv_t, w_all, cos_t, sin_t, cos_l, sin_l)
    att = _moba_attention(q_t, k, v_t, kmean.reshape(b, s // MOBA_BLOCK, ATTN_WIDTH))
    return _pool_merge_out(x, h, mod, g_final.reshape(1, d), w_all,
                           w_pool_grp[0].astype(_BF16), pool_scale[0].reshape(1, POOL_WIDTH),
                           w_pool_up[0].astype(_BF16), w_attn_up[0].astype(_BF16), w_out[0].astype(_BF16), att)
```

```python
import functools

import jax
import jax.numpy as jnp
from jax import lax
from jax.experimental import pallas as pl
from jax.experimental.pallas import tpu as pltpu

D_MODEL = 1024
POOL_WIDTH = 512
POOL_WINDOWS = (2, 4, 8, 16)
POOL_GC = 128
POOL_HIST = 16
HEAD_DIM = 64
N_HEADS = 8
ATTN_WIDTH = N_HEADS * HEAD_DIM
MOBA_BLOCK = 256
MOBA_TOPK = 3
ROT_DIM = 16
ROT_HALF = ROT_DIM // 2
ROPE_THETA = 500000.0
EPS = 1e-6

TM_QKV = 1024
TM_OUT = 512
VMEM_LIMIT_BYTES = 56 * 1024 * 1024

_Q_COL, _K_COL, _V_COL, _ZA_COL = 1024, 1536, 2048, 2560
IN_WIDTH = 5120

_BF16 = jnp.bfloat16
_F32 = jnp.float32
_NT_DIMS = (((1,), (1,)), ((), ()))


def _adaln_norm(x, g, mod_ref):
    ms = jnp.mean(x * x, axis=-1, keepdims=True)
    shift = mod_ref[0, 0:1, :]
    scl = mod_ref[0, 1:2, :]
    return x * lax.rsqrt(ms + EPS) * (g * (1.0 + scl)) + shift


def _mod_kernel(c_ref, w_ref, b_ref, o_ref):
    o_ref[...] = jnp.dot(c_ref[...], w_ref[...], preferred_element_type=_F32,
                         precision=lax.Precision.HIGHEST) + b_ref[...]


def _adaln_mod(c, w_ada, b_ada):
    nb_, d = c.shape
    rows = -(-nb_ // 8) * 8
    c_pad = jnp.zeros((rows, d), _F32).at[:nb_].set(c)
    n_out = w_ada.shape[1]
    out = pl.pallas_call(
        _mod_kernel,
        grid=(n_out // d,),
        in_specs=[pl.BlockSpec((rows, d), lambda j: (0, 0)),
                  pl.BlockSpec((d, d), lambda j: (0, j)),
                  pl.BlockSpec((1, d), lambda j: (0, j))],
        out_specs=pl.BlockSpec((rows, d), lambda j: (0, j)),
        out_shape=jax.ShapeDtypeStruct((rows, n_out), _F32),
        name="adaln_mod",
    )(c_pad, w_ada, b_ada.reshape(1, n_out))
    return out[:nb_].reshape(nb_, n_out // d, d)


def _qkv_kernel(x_ref, mod_ref, g_ref, wq_ref, wv_ref, wk_ref, cos_t_ref, sin_t_ref, cos_l_ref, sin_l_ref,
                qt_ref, k_ref, vt_ref, km_ref, h_ref):
    blocks = range(x_ref.shape[1] // MOBA_BLOCK)
    rows = [slice(i * MOBA_BLOCK, (i + 1) * MOBA_BLOCK) for i in blocks]
    h = [_adaln_norm(x_ref[0, rows[i], :], g_ref[...], mod_ref).astype(_BF16) for i in blocks]

    q_t, v_t, k = [], [], []
    for i in blocks:
        q_t.append(lax.dot_general(wq_ref[...], h[i], _NT_DIMS, preferred_element_type=_F32))
        k.append(jnp.dot(h[i], wk_ref[...], preferred_element_type=_F32))
        v_t.append(lax.dot_general(wv_ref[...], h[i], _NT_DIMS, preferred_element_type=_F32))

    dim = lax.broadcasted_iota(jnp.int32, (MOBA_BLOCK, ATTN_WIDTH), 1) % HEAD_DIM
    for i in blocks:
        h_ref[0, rows[i], :] = h[i]
        cos_t = cos_t_ref[:, rows[i]]
        sin_t = sin_t_ref[:, rows[i]]
        pieces = []
        for hd in range(N_HEADS):
            base = hd * HEAD_DIM
            x1 = q_t[i][base:base + ROT_HALF]
            x2 = q_t[i][base + ROT_HALF:base + ROT_DIM]
            pieces += [x1 * cos_t - x2 * sin_t, x2 * cos_t + x1 * sin_t,
                       q_t[i][base + ROT_DIM:base + HEAD_DIM]]
        qt_ref[0, i] = (jnp.concatenate(pieces, axis=0) * (HEAD_DIM ** -0.5 * LOG2_E)).astype(_BF16)
        vt_ref[0, i] = v_t[i].astype(_BF16)

        cos_l = jnp.concatenate([cos_l_ref[rows[i], :]] * (ATTN_WIDTH // 128), axis=1)
        sin_l = jnp.concatenate([sin_l_ref[rows[i], :]] * (ATTN_WIDTH // 128), axis=1)
        partner = jnp.where(dim < ROT_HALF,
                            pltpu.roll(k[i], ATTN_WIDTH - ROT_HALF, axis=1),
                            pltpu.roll(k[i], ROT_HALF, axis=1))
        kb = jnp.where(dim < ROT_DIM, k[i] * cos_l + partner * sin_l, k[i])
        km_ref[0, i] = jnp.mean(kb, axis=0, keepdims=True)
        k_ref[0, i] = kb.astype(_BF16)


def _qkv_proj(x, mod, g_norm, wq_t, wv_t, w_all, cos_t, sin_t, cos_l, sin_l):
    b, s, d = x.shape
    nb = s // MOBA_BLOCK
    tm = TM_QKV
    nbt = tm // MOBA_BLOCK
    full = lambda *shape: pl.BlockSpec(shape, lambda bi, ti: (0,) * len(shape))
    return pl.pallas_call(
        _qkv_kernel,
        grid=(b, s // tm),
        in_specs=[pl.BlockSpec((1, tm, d), lambda bi, ti: (bi, ti, 0)),
                  pl.BlockSpec((1, 3, d), lambda bi, ti: (bi, 0, 0)),
                  full(1, d),
                  full(ATTN_WIDTH, d),
                  full(ATTN_WIDTH, d),
                  pl.BlockSpec((d, ATTN_WIDTH), lambda bi, ti: (0, _K_COL // ATTN_WIDTH)),
                  pl.BlockSpec((ROT_HALF, tm), lambda bi, ti: (0, ti)),
                  pl.BlockSpec((ROT_HALF, tm), lambda bi, ti: (0, ti)),
                  pl.BlockSpec((tm, 128), lambda bi, ti: (ti, 0)),
                  pl.BlockSpec((tm, 128), lambda bi, ti: (ti, 0))],
        out_specs=[pl.BlockSpec((1, nbt, ATTN_WIDTH, MOBA_BLOCK), lambda bi, ti: (bi, ti, 0, 0)),
                   pl.BlockSpec((1, nbt, MOBA_BLOCK, ATTN_WIDTH), lambda bi, ti: (bi, ti, 0, 0)),
                   pl.BlockSpec((1, nbt, ATTN_WIDTH, MOBA_BLOCK), lambda bi, ti: (bi, ti, 0, 0)),
                   pl.BlockSpec((1, nbt, 1, ATTN_WIDTH), lambda bi, ti: (bi, ti, 0, 0)),
                   pl.BlockSpec((1, tm, d), lambda bi, ti: (bi, ti, 0))],
        out_shape=[jax.ShapeDtypeStruct((b, nb, ATTN_WIDTH, MOBA_BLOCK), _BF16),
                   jax.ShapeDtypeStruct((b, nb, MOBA_BLOCK, ATTN_WIDTH), _BF16),
                   jax.ShapeDtypeStruct((b, nb, ATTN_WIDTH, MOBA_BLOCK), _BF16),
                   jax.ShapeDtypeStruct((b, nb, 1, ATTN_WIDTH), _F32),
                   jax.ShapeDtypeStruct((b, s, d), _BF16)],
        compiler_params=pltpu.CompilerParams(dimension_semantics=("arbitrary", "arbitrary"),
                                             vmem_limit_bytes=VMEM_LIMIT_BYTES),
        name="qkv_proj",
    )(x, mod, g_norm, wq_t, wv_t, w_all, cos_t, sin_t, cos_l, sin_l)


V_EXT = HEAD_DIM + 16
QK_AHEAD = 4
PV_BEHIND = 1
Q_SUB = 2
N_LANES = Q_SUB * N_HEADS
KV_LONG = 4
KV_SHORT = 2
LOG2_E = 1.4426950408889634


def _attn_kernel(qt_ref, k_ref, vt_ref, km_ref, o_ref, qh_ref, sel_ref, acc_ref, s_ref):
    i0 = pl.program_id(1) * Q_SUB
    nb = k_ref.shape[1]
    blk = MOBA_BLOCK
    neg_inf = -jnp.inf
    lanes = [(qs, hd) for qs in range(Q_SUB) for hd in range(N_HEADS)]

    km = km_ref[0].astype(_BF16)
    k_own = [k_ref[0, i0 + qs] for qs in range(Q_SUB)]
    v_own = [vt_ref[0, i0 + qs] for qs in range(Q_SUB)]
    pair_row = lax.broadcasted_iota(jnp.int32, (2 * HEAD_DIM, blk), 0)
    blk_id = lax.broadcasted_iota(jnp.int32, (nb, blk), 0)
    key_pos = lax.broadcasted_iota(jnp.int32, (blk, blk), 0)
    qry_pos = lax.broadcasted_iota(jnp.int32, (blk, blk), 1)
    ones = jnp.ones((V_EXT - HEAD_DIM, blk), _BF16)

    def pair_cols(arr, hd):
        p0 = (hd // 2) * 2 * HEAD_DIM
        return arr[:, p0:p0 + 2 * HEAD_DIM]

    def v_ext(v_blk, hd):
        return jnp.concatenate([v_blk[hd * HEAD_DIM:(hd + 1) * HEAD_DIM], ones], axis=0)

    def software_pipeline(items, score, softmax, accumulate):
        n_items = len(items)
        scores, probs = {}, {}
        for step in range(-QK_AHEAD, n_items + PV_BEHIND):
            if 0 <= step + QK_AHEAD < n_items:
                scores[step + QK_AHEAD] = score(items[step + QK_AHEAD])
            if 0 <= step < n_items:
                probs[step] = softmax(items[step], scores.pop(step))
            if 0 <= step - PV_BEHIND < n_items:
                accumulate(items[step - PV_BEHIND], probs.pop(step - PV_BEHIND))

    for ln, (qs, hd) in enumerate(lanes):
        p0 = (hd // 2) * 2 * HEAD_DIM
        in_head = (pair_row >= HEAD_DIM) == (hd % 2 == 1)
        qh_ref[ln] = jnp.where(in_head, qt_ref[0, qs, p0:p0 + 2 * HEAD_DIM, :], jnp.zeros((), _BF16))

    def gate_scores(ln):
        return jnp.dot(pair_cols(km, lanes[ln][1]), qh_ref[ln], preferred_element_type=_F32)

    def select_blocks(ln, gate):
        past = blk_id < i0 + lanes[ln][0]
        gate = jnp.where(past, gate, neg_inf)
        sel = jnp.zeros((nb, blk), _F32)
        for _ in range(MOBA_TOPK):
            best = jnp.max(gate, axis=0, keepdims=True)
            idx = jnp.min(jnp.where(gate == best, blk_id, nb), axis=0, keepdims=True)
            pick = blk_id == idx
            sel = jnp.where(pick, 1.0, sel)
            gate = jnp.where(pick, neg_inf, gate)
        sel_ref[ln] = jnp.where(past, sel, 0.0)

    software_pipeline(list(range(N_LANES)), gate_scores, select_blocks, lambda ln, _: None)

    def qk(k_blk, ln):
        return jnp.dot(pair_cols(k_blk, lanes[ln][1]), qh_ref[ln], preferred_element_type=_F32)

    def past_softmax(ln, j, s, m_cur):
        chosen = sel_ref[ln, pl.ds(j, 1), :] > 0.5
        m_old = m_cur[ln]
        m_new = jnp.where(chosen, jnp.maximum(m_old, jnp.max(s, axis=0, keepdims=True)), m_old)
        m_cur[ln] = m_new
        shift = jnp.where(chosen, m_new, jnp.inf)
        return jnp.exp2(s - shift).astype(_BF16), jnp.exp2(m_old - m_new)

    def accumulate(ln, v_blk, p, alpha):
        acc_ref[ln] = alpha * acc_ref[ln] + jnp.dot(v_ext(v_blk, lanes[ln][1]), p, preferred_element_type=_F32)

    handover = [("next", n) for n in range(QK_AHEAD)]

    m_cur = [None] * N_LANES
    own_items = [("own", ln, qs) for ln, (qs, _) in enumerate(lanes)]
    own_items += [("past", ln, jb) for jb in range(Q_SUB) for ln, (qs, _) in enumerate(lanes) if jb < qs]

    def own_score(item):
        kind, ln = item[:2]
        return qk(k_ref[0, 0], ln) if kind == "next" else qk(k_own[item[2]], ln)

    def own_softmax(item, s):
        kind, ln = item[:2]
        if kind == "next":
            s_ref[ln] = s
            return None
        if kind == "past":
            return past_softmax(ln, i0 + item[2], s, m_cur)
        s = jnp.where(key_pos <= qry_pos, s, neg_inf)
        m_cur[ln] = jnp.max(s, axis=0, keepdims=True)
        return jnp.exp2(s - m_cur[ln]).astype(_BF16), None

    def own_accumulate(item, p_alpha):
        kind, ln = item[:2]
        if kind == "own":
            acc_ref[ln] = jnp.dot(v_ext(v_own[item[2]], lanes[ln][1]), p_alpha[0], preferred_element_type=_F32)
        elif kind == "past":
            accumulate(ln, v_own[item[2]], *p_alpha)

    software_pipeline(own_items + handover, own_score, own_softmax, own_accumulate)

    def tile_of(n):
        return divmod(n, N_LANES)

    def past_blocks(n_blocks, first, t, m_run):
        j0 = first + t * n_blocks
        j_next = jnp.minimum(j0 + n_blocks, nb - 1)
        k_j = [k_ref[0, j0 + u] for u in range(n_blocks)]
        v_j = [vt_ref[0, j0 + u] for u in range(n_blocks)]
        m_trip = list(m_run)

        def score(item):
            kind, n = item
            u, ln = tile_of(n)
            if kind == "next":
                return qk(k_ref[0, j_next], n)
            return s_ref[n] if n < QK_AHEAD else qk(k_j[u], ln)

        def softmax(item, s):
            kind, n = item
            if kind == "next":
                s_ref[n] = s
                return None
            u, ln = tile_of(n)
            return past_softmax(ln, j0 + u, s, m_trip)

        def finish(item, p_alpha):
            kind, n = item
            if kind == "tile":
                u, ln = tile_of(n)
                accumulate(ln, v_j[u], *p_alpha)

        software_pipeline([("tile", n) for n in range(n_blocks * N_LANES)] + handover, score, softmax, finish)
        return tuple(m_trip)

    n_long = i0 // KV_LONG
    n_short = (i0 - n_long * KV_LONG + KV_SHORT - 1) // KV_SHORT
    m_run = lax.fori_loop(0, n_long, functools.partial(past_blocks, KV_LONG, 0), tuple(m_cur))
    lax.fori_loop(0, n_short, functools.partial(past_blocks, KV_SHORT, n_long * KV_LONG), m_run)

    for qs in range(Q_SUB):
        outs = []
        for ln in range(qs * N_HEADS, (qs + 1) * N_HEADS):
            acc = acc_ref[ln]
            outs.append(acc[:HEAD_DIM] / acc[HEAD_DIM:HEAD_DIM + 1])
        o_ref[0, qs * blk:(qs + 1) * blk, :] = jnp.concatenate(outs, axis=0).T.astype(o_ref.dtype)


def _moba_attention(q_t, k, v_t, kmean):
    b, nb = k.shape[0], k.shape[1]
    s = nb * MOBA_BLOCK
    assert nb % Q_SUB == 0 and Q_SUB % KV_SHORT == 0 and QK_AHEAD <= N_HEADS
    return pl.pallas_call(
        _attn_kernel,
        grid=(b, nb // Q_SUB),
        in_specs=[pl.BlockSpec((1, Q_SUB, ATTN_WIDTH, MOBA_BLOCK), lambda bi, qi: (bi, qi, 0, 0)),
                  pl.BlockSpec((1, nb, MOBA_BLOCK, ATTN_WIDTH), lambda bi, qi: (bi, 0, 0, 0)),
                  pl.BlockSpec((1, nb, ATTN_WIDTH, MOBA_BLOCK), lambda bi, qi: (bi, 0, 0, 0)),
                  pl.BlockSpec((1, nb, ATTN_WIDTH), lambda bi, qi: (bi, 0, 0))],
        out_specs=pl.BlockSpec((1, Q_SUB * MOBA_BLOCK, ATTN_WIDTH), lambda bi, qi: (bi, qi, 0)),
        out_shape=jax.ShapeDtypeStruct((b, s, ATTN_WIDTH), _BF16),
        scratch_shapes=[pltpu.VMEM((N_LANES, 2 * HEAD_DIM, MOBA_BLOCK), _BF16),
                        pltpu.VMEM((N_LANES, nb, MOBA_BLOCK), _F32),
                        pltpu.VMEM((N_LANES, V_EXT, MOBA_BLOCK), _F32),
                        pltpu.VMEM((QK_AHEAD, MOBA_BLOCK, MOBA_BLOCK), _F32)],
        compiler_params=pltpu.CompilerParams(dimension_semantics=("arbitrary", "arbitrary"),
                                             vmem_limit_bytes=VMEM_LIMIT_BYTES),
        name="moba_attention",
    )(q_t, k, v_t, kmean)


_HEAD_W = _Q_COL
_TAIL_W = IN_WIDTH - _ZA_COL
SUB_ROWS = 256


def _out_kernel(x_ref, h_ref, mod_ref, gf_ref, wh_ref, wt_ref, wg_ref, ps_ref, wpu_ref, wau_ref, wo_ref, att_ref,
                out_ref, hist_ref, carry_ref):
    t = pl.program_id(1)
    tm = x_ref.shape[1]
    subs = range(tm // SUB_ROWS)
    rows = [slice(s * SUB_ROWS, (s + 1) * SUB_ROWS) for s in subs]

    def mm(a, w):
        return jnp.dot(a.astype(_BF16), w, preferred_element_type=_F32)

    @pl.when(t == 0)
    def _():
        carry_ref[...] = jnp.zeros_like(carry_ref)

    h = [h_ref[0, r, :] for r in rows]
    u = [mm(h[s], wh_ref[:, :POOL_WIDTH]) for s in subs]
    z_attn = [mm(h[s], wt_ref[:, :ATTN_WIDTH]) for s in subs]
    z_pool = [mm(h[s], wh_ref[:, POOL_WIDTH:]) for s in subs]
    m_attn = [mm(h[s], wt_ref[:, ATTN_WIDTH + D_MODEL:]) for s in subs]
    m_pool = [mm(h[s], wt_ref[:, ATTN_WIDTH:ATTN_WIDTH + D_MODEL]) for s in subs]

    hist_ref[0:POOL_HIST, :] = carry_ref[...]
    for s in subs:
        hist_ref[POOL_HIST + s * SUB_ROWS:POOL_HIST + (s + 1) * SUB_ROWS, :] = u[s]
    carry_ref[...] = u[-1][SUB_ROWS - POOL_HIST:, :]

    y_attn = [mm(att_ref[0, rows[s], :].astype(_F32) * jax.nn.silu(z_attn[s]), wau_ref[...]) for s in subs]

    y_grp = []
    for s in subs:
        r0 = POOL_HIST + s * SUB_ROWS
        pos = t * tm + s * SUB_ROWS + lax.broadcasted_iota(jnp.int32, (SUB_ROWS, POOL_GC), 0)
        ys = []
        for g, w in enumerate(POOL_WINDOWS):
            lanes = slice(g * POOL_GC, (g + 1) * POOL_GC)
            acc = hist_ref[r0:r0 + SUB_ROWS, lanes]
            for back in range(1, w):
                acc = acc + hist_ref[r0 - back:r0 - back + SUB_ROWS, lanes]
            cnt = jnp.minimum(pos + 1, w).astype(_F32)
            ys.append(mm(acc / cnt - u[s][:, lanes], wg_ref[g]))
        y_grp.append(jnp.concatenate(ys, axis=1))
    y_pool = [mm(y_grp[s] * ps_ref[...] * jax.nn.silu(z_pool[s]), wpu_ref[...]) for s in subs]

    res = [mm(jax.nn.sigmoid(m_pool[s]) * y_pool[s] + jax.nn.sigmoid(m_attn[s]) * y_attn[s], wo_ref[...])
           for s in subs]
    for s in subs:
        xn = x_ref[0, rows[s], :] + mod_ref[0, 2:3, :] * res[s]
        ms = jnp.mean(xn * xn, axis=-1, keepdims=True)
        out_ref[0, rows[s], :] = xn * lax.rsqrt(ms + EPS) * gf_ref[...]


def _pool_merge_out(x, h, mod, g_final, w_all, w_grp, pool_scale, w_pool_up, w_attn_up, w_out, att):
    assert _ZA_COL % _TAIL_W == 0
    b, s, d = x.shape
    tm = TM_OUT
    full = lambda *shape: pl.BlockSpec(shape, lambda bi, ti: (0,) * len(shape))
    return pl.pallas_call(
        _out_kernel,
        grid=(b, s // tm),
        in_specs=[pl.BlockSpec((1, tm, d), lambda bi, ti: (bi, ti, 0)),
                  pl.BlockSpec((1, tm, d), lambda bi, ti: (bi, ti, 0)),
                  pl.BlockSpec((1, 3, d), lambda bi, ti: (bi, 0, 0)),
                  full(1, d),
                  pl.BlockSpec((d, _HEAD_W), lambda bi, ti: (0, 0)),
                  pl.BlockSpec((d, _TAIL_W), lambda bi, ti: (0, _ZA_COL // _TAIL_W)),
                  full(len(POOL_WINDOWS), POOL_GC, POOL_GC),
                  full(1, POOL_WIDTH),
                  full(POOL_WIDTH, d), full(ATTN_WIDTH, d), full(d, d),
                  pl.BlockSpec((1, tm, ATTN_WIDTH), lambda bi, ti: (bi, ti, 0))],
        out_specs=pl.BlockSpec((1, tm, d), lambda bi, ti: (bi, ti, 0)),
        out_shape=jax.ShapeDtypeStruct((b, s, d), _F32),
        scratch_shapes=[pltpu.VMEM((POOL_HIST + tm, POOL_WIDTH), _F32),
                        pltpu.VMEM((POOL_HIST, POOL_WIDTH), _F32)],
        compiler_params=pltpu.CompilerParams(dimension_semantics=("arbitrary", "arbitrary"),
                                             vmem_limit_bytes=VMEM_LIMIT_BYTES),
        name="pool_merge_out",
    )(x, h, mod, g_final, w_all, w_all, w_grp, pool_scale, w_pool_up, w_attn_up, w_out, att)


def _rope_tables(s):
    pos = jnp.arange(s, dtype=_F32)
    inv_freq = ROPE_THETA ** (-jnp.arange(0, ROT_DIM, 2, dtype=_F32) / ROT_DIM)
    ang_t = inv_freq[:, None] * pos[None, :]
    dim = jnp.arange(128) % HEAD_DIM
    rot = dim < ROT_DIM
    ang_l = pos[:, None] * inv_freq[dim % ROT_HALF][None, :]
    cos_l = jnp.where(rot, jnp.cos(ang_l), 1.0)
    sin_l = jnp.where(rot, jnp.where(dim < ROT_HALF, -1.0, 1.0) * jnp.sin(ang_l), 0.0)
    return jnp.cos(ang_t), jnp.sin(ang_t), cos_l, sin_l


def kernel(x, c, w_ada, b_ada, g_norm, w_in, w_pool_grp, pool_scale, w_pool_up, w_attn_up, w_out, g_final):
    b, s, d = x.shape
    assert w_ada.shape[0] == 1, "single-layer configuration only"
    assert d == D_MODEL and s % TM_QKV == 0 and s % TM_OUT == 0
    assert w_in.shape[1:] == (d, IN_WIDTH)
    w_all = w_in[0].astype(_BF16)
    wq_t = w_all[:, _Q_COL:_K_COL].T
    wv_t = w_all[:, _V_COL:_ZA_COL].T

    mod = _adaln_mod(c, w_ada[0], b_ada[0])
    cos_t, sin_t, cos_l, sin_l = _rope_tables(s)
    g1 = g_norm[0].reshape(1, d)
    q_t, k, v_t, kmean, h = _qkv_proj(x, mod, g1, wq_t, wv_t, w_all, cos_t, sin_t, cos_l, sin_l)
    att = _moba_attention(q_t, k, v_t, kmean.reshape(b, s // MOBA_BLOCK, ATTN_WIDTH))
    return _pool_merge_out(x, h, mod, g_final.reshape(1, d), w_all,
                           w_pool_grp[0].astype(_BF16), pool_scale[0].reshape(1, POOL_WIDTH),
                           w_pool_up[0].astype(_BF16), w_attn_up[0].astype(_BF16), w_out[0].astype(_BF16), att)
```

```python
import functools

import jax
import jax.numpy as jnp
from jax import lax
from jax.experimental import pallas as pl
from jax.experimental.pallas import tpu as pltpu

D_MODEL = 1024
POOL_WIDTH = 512
POOL_WINDOWS = (2, 4, 8, 16)
POOL_GC = 128
POOL_HIST = 8 * len(POOL_WINDOWS)
assert all(w == 2 ** (g + 1) for g, w in enumerate(POOL_WINDOWS))
HEAD_DIM = 64
N_HEADS = 8
ATTN_WIDTH = N_HEADS * HEAD_DIM
MOBA_BLOCK = 256
MOBA_TOPK = 3
ROT_DIM = 16
ROT_HALF = ROT_DIM // 2
ROPE_THETA = 500000.0
EPS = 1e-6

TM_QKV = 1024
TM_OUT = 512
VMEM_LIMIT_BYTES = 56 * 1024 * 1024

_Q_COL, _K_COL, _V_COL, _ZA_COL = 1024, 1536, 2048, 2560
IN_WIDTH = 5120

_BF16 = jnp.bfloat16
_F32 = jnp.float32
_NT_DIMS = (((1,), (1,)), ((), ()))


def _adaln_norm(x, g, mod_ref):
    ms = jnp.mean(x * x, axis=-1, keepdims=True)
    shift = mod_ref[0, 0:1, :]
    scl = mod_ref[0, 1:2, :]
    return x * lax.rsqrt(ms + EPS) * (g * (1.0 + scl)) + shift


def _mod_kernel(c_ref, w_ref, b_ref, o_ref):
    o_ref[...] = jnp.dot(c_ref[...].astype(_BF16), w_ref[...].astype(_BF16),
                         preferred_element_type=_F32) + b_ref[...]


def _adaln_mod(c, w_ada, b_ada):
    nb_, d = c.shape
    rows = -(-nb_ // 8) * 8
    c_pad = jnp.zeros((rows, d), _F32).at[:nb_].set(c)
    n_out = w_ada.shape[1]
    out = pl.pallas_call(
        _mod_kernel,
        grid=(n_out // d,),
        in_specs=[pl.BlockSpec((rows, d), lambda j: (0, 0)),
                  pl.BlockSpec((d, d), lambda j: (0, j)),
                  pl.BlockSpec((1, d), lambda j: (0, j))],
        out_specs=pl.BlockSpec((rows, d), lambda j: (0, j)),
        out_shape=jax.ShapeDtypeStruct((rows, n_out), _F32),
        name="adaln_mod",
    )(c_pad, w_ada, b_ada.reshape(1, n_out))
    return out[:nb_].reshape(nb_, n_out // d, d)


def _qkv_kernel(x_ref, mod_ref, g_ref, wq_ref, wv_ref, wk_ref, cos_t_ref, sin_t_ref, cos_l_ref, sin_l_ref,
                qt_ref, k_ref, vt_ref, km_ref, h_ref):
    blocks = range(x_ref.shape[1] // MOBA_BLOCK)
    rows = [slice(i * MOBA_BLOCK, (i + 1) * MOBA_BLOCK) for i in blocks]
    h = [_adaln_norm(x_ref[0, rows[i], :], g_ref[...], mod_ref).astype(_BF16) for i in blocks]

    q_t, v_t, k = [], [], []
    for i in blocks:
        q_t.append(lax.dot_general(wq_ref[...], h[i], _NT_DIMS, preferred_element_type=_F32))
        k.append(jnp.dot(h[i], wk_ref[...], preferred_element_type=_F32))
        v_t.append(lax.dot_general(wv_ref[...], h[i], _NT_DIMS, preferred_element_type=_F32))

    dim = lax.broadcasted_iota(jnp.int32, (MOBA_BLOCK, ATTN_WIDTH), 1) % HEAD_DIM
    for i in blocks:
        h_ref[0, rows[i], :] = h[i]
        cos_t = cos_t_ref[:, rows[i]]
        sin_t = sin_t_ref[:, rows[i]]
        pieces = []
        for hd in range(N_HEADS):
            base = hd * HEAD_DIM
            x1 = q_t[i][base:base + ROT_HALF]
            x2 = q_t[i][base + ROT_HALF:base + ROT_DIM]
            pieces += [x1 * cos_t - x2 * sin_t, x2 * cos_t + x1 * sin_t,
                       q_t[i][base + ROT_DIM:base + HEAD_DIM]]
        qt_ref[0, i] = (jnp.concatenate(pieces, axis=0) * (HEAD_DIM ** -0.5 * LOG2_E)).astype(_BF16)
        vt_ref[0, i] = v_t[i].astype(_BF16)

        cos_l = jnp.concatenate([cos_l_ref[rows[i], :]] * (ATTN_WIDTH // 128), axis=1)
        sin_l = jnp.concatenate([sin_l_ref[rows[i], :]] * (ATTN_WIDTH // 128), axis=1)
        partner = jnp.where(dim < ROT_HALF,
                            pltpu.roll(k[i], ATTN_WIDTH - ROT_HALF, axis=1),
                            pltpu.roll(k[i], ROT_HALF, axis=1))
        kb = jnp.where(dim < ROT_DIM, k[i] * cos_l + partner * sin_l, k[i])
        km_ref[0, i] = jnp.mean(kb, axis=0, keepdims=True)
        k_ref[0, i] = kb.astype(_BF16)


def _qkv_proj(x, mod, g_norm, wq_t, wv_t, w_all, cos_t, sin_t, cos_l, sin_l):
    b, s, d = x.shape
    nb = s // MOBA_BLOCK
    tm = TM_QKV
    nbt = tm // MOBA_BLOCK
    full = lambda *shape: pl.BlockSpec(shape, lambda bi, ti: (0,) * len(shape))
    return pl.pallas_call(
        _qkv_kernel,
        grid=(b, s // tm),
        in_specs=[pl.BlockSpec((1, tm, d), lambda bi, ti: (bi, ti, 0)),
                  pl.BlockSpec((1, 3, d), lambda bi, ti: (bi, 0, 0)),
                  full(1, d),
                  full(ATTN_WIDTH, d),
                  full(ATTN_WIDTH, d),
                  pl.BlockSpec((d, ATTN_WIDTH), lambda bi, ti: (0, _K_COL // ATTN_WIDTH)),
                  pl.BlockSpec((ROT_HALF, tm), lambda bi, ti: (0, ti)),
                  pl.BlockSpec((ROT_HALF, tm), lambda bi, ti: (0, ti)),
                  pl.BlockSpec((tm, 128), lambda bi, ti: (ti, 0)),
                  pl.BlockSpec((tm, 128), lambda bi, ti: (ti, 0))],
        out_specs=[pl.BlockSpec((1, nbt, ATTN_WIDTH, MOBA_BLOCK), lambda bi, ti: (bi, ti, 0, 0)),
                   pl.BlockSpec((1, nbt, MOBA_BLOCK, ATTN_WIDTH), lambda bi, ti: (bi, ti, 0, 0)),
                   pl.BlockSpec((1, nbt, ATTN_WIDTH, MOBA_BLOCK), lambda bi, ti: (bi, ti, 0, 0)),
                   pl.BlockSpec((1, nbt, 1, ATTN_WIDTH), lambda bi, ti: (bi, ti, 0, 0)),
                   pl.BlockSpec((1, tm, d), lambda bi, ti: (bi, ti, 0))],
        out_shape=[jax.ShapeDtypeStruct((b, nb, ATTN_WIDTH, MOBA_BLOCK), _BF16),
                   jax.ShapeDtypeStruct((b, nb, MOBA_BLOCK, ATTN_WIDTH), _BF16),
                   jax.ShapeDtypeStruct((b, nb, ATTN_WIDTH, MOBA_BLOCK), _BF16),
                   jax.ShapeDtypeStruct((b, nb, 1, ATTN_WIDTH), _F32),
                   jax.ShapeDtypeStruct((b, s, d), _BF16)],
        compiler_params=pltpu.CompilerParams(dimension_semantics=("arbitrary", "arbitrary"),
                                             vmem_limit_bytes=VMEM_LIMIT_BYTES),
        name="qkv_proj",
    )(x, mod, g_norm, wq_t, wv_t, w_all, cos_t, sin_t, cos_l, sin_l)


V_EXT = HEAD_DIM + 16
QK_AHEAD = 4
PV_BEHIND = 1
Q_SUB = 2
N_LANES = Q_SUB * N_HEADS
KV_LONG = 4
KV_SHORT = 2
LOG2_E = 1.4426950408889634


def _attn_kernel(qt_ref, k_ref, vt_ref, km_ref, o_ref, qh_ref, sel_ref, acc_ref, s_ref):
    i0 = pl.program_id(1) * Q_SUB
    nb = k_ref.shape[1]
    blk = MOBA_BLOCK
    neg_inf = -jnp.inf
    lanes = [(qs, hd) for qs in range(Q_SUB) for hd in range(N_HEADS)]

    km = km_ref[0].astype(_BF16)
    k_own = [k_ref[0, i0 + qs] for qs in range(Q_SUB)]
    v_own = [vt_ref[0, i0 + qs] for qs in range(Q_SUB)]
    pair_row = lax.broadcasted_iota(jnp.int32, (2 * HEAD_DIM, blk), 0)
    blk_id = lax.broadcasted_iota(jnp.int32, (nb, blk), 0)
    key_pos = lax.broadcasted_iota(jnp.int32, (blk, blk), 0)
    qry_pos = lax.broadcasted_iota(jnp.int32, (blk, blk), 1)
    ones = jnp.ones((V_EXT - HEAD_DIM, blk), _BF16)

    def pair_cols(arr, hd):
        p0 = (hd // 2) * 2 * HEAD_DIM
        return arr[:, p0:p0 + 2 * HEAD_DIM]

    def v_ext(v_blk, hd):
        return jnp.concatenate([v_blk[hd * HEAD_DIM:(hd + 1) * HEAD_DIM], ones], axis=0)

    def software_pipeline(items, score, softmax, accumulate):
        n_items = len(items)
        scores, probs = {}, {}
        for step in range(-QK_AHEAD, n_items + PV_BEHIND):
            if 0 <= step + QK_AHEAD < n_items:
                scores[step + QK_AHEAD] = score(items[step + QK_AHEAD])
            if 0 <= step < n_items:
                probs[step] = softmax(items[step], scores.pop(step))
            if 0 <= step - PV_BEHIND < n_items:
                accumulate(items[step - PV_BEHIND], probs.pop(step - PV_BEHIND))

    for ln, (qs, hd) in enumerate(lanes):
        p0 = (hd // 2) * 2 * HEAD_DIM
        in_head = (pair_row >= HEAD_DIM) == (hd % 2 == 1)
        qh_ref[ln] = jnp.where(in_head, qt_ref[0, qs, p0:p0 + 2 * HEAD_DIM, :], jnp.zeros((), _BF16))

    def gate_scores(ln):
        return jnp.dot(pair_cols(km, lanes[ln][1]), qh_ref[ln], preferred_element_type=_F32)

    def select_blocks(ln, gate):
        past = blk_id < i0 + lanes[ln][0]
        gate = jnp.where(past, gate, neg_inf)
        sel = jnp.zeros((nb, blk), _F32)
        for _ in range(MOBA_TOPK):
            best = jnp.max(gate, axis=0, keepdims=True)
            idx = jnp.min(jnp.where(gate == best, blk_id, nb), axis=0, keepdims=True)
            pick = blk_id == idx
            sel = jnp.where(pick, 1.0, sel)
            gate = jnp.where(pick, neg_inf, gate)
        sel_ref[ln] = jnp.where(past, sel, 0.0)

    software_pipeline(list(range(N_LANES)), gate_scores, select_blocks, lambda ln, _: None)

    def qk(k_blk, ln):
        return jnp.dot(pair_cols(k_blk, lanes[ln][1]), qh_ref[ln], preferred_element_type=_F32)

    def past_softmax(ln, j, s, m_cur):
        chosen = sel_ref[ln, pl.ds(j, 1), :] > 0.5
        m_old = m_cur[ln]
        m_new = jnp.where(chosen, jnp.maximum(m_old, jnp.max(s, axis=0, keepdims=True)), m_old)
        m_cur[ln] = m_new
        shift = jnp.where(chosen, m_new, jnp.inf)
        return jnp.exp2(s - shift).astype(_BF16), jnp.exp2(m_old - m_new)

    def accumulate(ln, v_blk, p, alpha):
        acc_ref[ln] = alpha * acc_ref[ln] + jnp.dot(v_ext(v_blk, lanes[ln][1]), p, preferred_element_type=_F32)

    handover = [("next", n) for n in range(QK_AHEAD)]

    m_cur = [None] * N_LANES
    own_items = [("own", ln, qs) for ln, (qs, _) in enumerate(lanes)]
    own_items += [("past", ln, jb) for jb in range(Q_SUB) for ln, (qs, _) in enumerate(lanes) if jb < qs]

    def own_score(item):
        kind, ln = item[:2]
        return qk(k_ref[0, 0], ln) if kind == "next" else qk(k_own[item[2]], ln)

    def own_softmax(item, s):
        kind, ln = item[:2]
        if kind == "next":
            s_ref[ln] = s
            return None
        if kind == "past":
            return past_softmax(ln, i0 + item[2], s, m_cur)
        s = jnp.where(key_pos <= qry_pos, s, neg_inf)
        m_cur[ln] = jnp.max(s, axis=0, keepdims=True)
        return jnp.exp2(s - m_cur[ln]).astype(_BF16), None

    def own_accumulate(item, p_alpha):
        kind, ln = item[:2]
        if kind == "own":
            acc_ref[ln] = jnp.dot(v_ext(v_own[item[2]], lanes[ln][1]), p_alpha[0], preferred_element_type=_F32)
        elif kind == "past":
            accumulate(ln, v_own[item[2]], *p_alpha)

    software_pipeline(own_items + handover, own_score, own_softmax, own_accumulate)

    def tile_of(n):
        return divmod(n, N_LANES)

    def past_blocks(n_blocks, first, t, m_run):
        j0 = first + t * n_blocks
        j_next = jnp.minimum(j0 + n_blocks, nb - 1)
        k_j = [k_ref[0, j0 + u] for u in range(n_blocks)]
        v_j = [vt_ref[0, j0 + u] for u in range(n_blocks)]
        m_trip = list(m_run)

        def score(item):
            kind, n = item
            u, ln = tile_of(n)
            if kind == "next":
                return qk(k_ref[0, j_next], n)
            return s_ref[n] if n < QK_AHEAD else qk(k_j[u], ln)

        def softmax(item, s):
            kind, n = item
            if kind == "next":
                s_ref[n] = s
                return None
            u, ln = tile_of(n)
            return past_softmax(ln, j0 + u, s, m_trip)

        def finish(item, p_alpha):
            kind, n = item
            if kind == "tile":
                u, ln = tile_of(n)
                accumulate(ln, v_j[u], *p_alpha)

        software_pipeline([("tile", n) for n in range(n_blocks * N_LANES)] + handover, score, softmax, finish)
        return tuple(m_trip)

    n_long = i0 // KV_LONG
    n_short = (i0 - n_long * KV_LONG + KV_SHORT - 1) // KV_SHORT
    m_run = lax.fori_loop(0, n_long, functools.partial(past_blocks, KV_LONG, 0), tuple(m_cur))
    lax.fori_loop(0, n_short, functools.partial(past_blocks, KV_SHORT, n_long * KV_LONG), m_run)

    for qs in range(Q_SUB):
        outs = []
        for ln in range(qs * N_HEADS, (qs + 1) * N_HEADS):
            acc = acc_ref[ln]
            outs.append(acc[:HEAD_DIM] / acc[HEAD_DIM:HEAD_DIM + 1])
        o_ref[0, qs * blk:(qs + 1) * blk, :] = jnp.concatenate(outs, axis=0).T.astype(o_ref.dtype)


def _moba_attention(q_t, k, v_t, kmean):
    b, nb = k.shape[0], k.shape[1]
    s = nb * MOBA_BLOCK
    assert nb % Q_SUB == 0 and Q_SUB % KV_SHORT == 0 and QK_AHEAD <= N_HEADS
    return pl.pallas_call(
        _attn_kernel,
        grid=(b, nb // Q_SUB),
        in_specs=[pl.BlockSpec((1, Q_SUB, ATTN_WIDTH, MOBA_BLOCK), lambda bi, qi: (bi, qi, 0, 0)),
                  pl.BlockSpec((1, nb, MOBA_BLOCK, ATTN_WIDTH), lambda bi, qi: (bi, 0, 0, 0)),
                  pl.BlockSpec((1, nb, ATTN_WIDTH, MOBA_BLOCK), lambda bi, qi: (bi, 0, 0, 0)),
                  pl.BlockSpec((1, nb, ATTN_WIDTH), lambda bi, qi: (bi, 0, 0))],
        out_specs=pl.BlockSpec((1, Q_SUB * MOBA_BLOCK, ATTN_WIDTH), lambda bi, qi: (bi, qi, 0)),
        out_shape=jax.ShapeDtypeStruct((b, s, ATTN_WIDTH), _BF16),
        scratch_shapes=[pltpu.VMEM((N_LANES, 2 * HEAD_DIM, MOBA_BLOCK), _BF16),
                        pltpu.VMEM((N_LANES, nb, MOBA_BLOCK), _F32),
                        pltpu.VMEM((N_LANES, V_EXT, MOBA_BLOCK), _F32),
                        pltpu.VMEM((QK_AHEAD, MOBA_BLOCK, MOBA_BLOCK), _F32)],
        compiler_params=pltpu.CompilerParams(dimension_semantics=("arbitrary", "arbitrary"),
                                             vmem_limit_bytes=VMEM_LIMIT_BYTES),
        name="moba_attention",
    )(q_t, k, v_t, kmean)


_HEAD_W = _Q_COL
_TAIL_W = IN_WIDTH - _ZA_COL
SUB_ROWS = 256


def _out_kernel(x_ref, h_ref, mod_ref, gf_ref, wh_ref, wt_ref, wg_ref, ps_ref, wpu_ref, wau_ref, wo_ref, att_ref,
                out_ref, hist_ref, carry_ref, lv_ref):
    t = pl.program_id(1)
    tm = x_ref.shape[1]
    subs = range(tm // SUB_ROWS)
    rows = [slice(s * SUB_ROWS, (s + 1) * SUB_ROWS) for s in subs]

    def mm(a, w):
        return jnp.dot(a.astype(_BF16), w, preferred_element_type=_F32)

    @pl.when(t == 0)
    def _():
        carry_ref[...] = jnp.zeros_like(carry_ref)

    h = [h_ref[0, r, :] for r in rows]
    u = [mm(h[s], wh_ref[:, :POOL_WIDTH]) for s in subs]
    z_attn = [mm(h[s], wt_ref[:, :ATTN_WIDTH]) for s in subs]
    z_pool = [mm(h[s], wh_ref[:, POOL_WIDTH:]) for s in subs]
    m_attn = [mm(h[s], wt_ref[:, ATTN_WIDTH + D_MODEL:]) for s in subs]
    m_pool = [mm(h[s], wt_ref[:, ATTN_WIDTH:ATTN_WIDTH + D_MODEL]) for s in subs]

    hist_ref[0:POOL_HIST, :] = carry_ref[...]
    for s in subs:
        hist_ref[POOL_HIST + s * SUB_ROWS:POOL_HIST + (s + 1) * SUB_ROWS, :] = u[s]
    carry_ref[...] = u[-1][SUB_ROWS - POOL_HIST:, :]

    y_attn = [mm(att_ref[0, rows[s], :].astype(_F32) * jax.nn.silu(z_attn[s]), wau_ref[...]) for s in subs]

    end = POOL_HIST + tm

    def level(k, r0, r1, lanes):
        return hist_ref[r0:r1, lanes] if k == 0 else lv_ref[k - 1, r0:r1, lanes]

    for k in range(1, len(POOL_WINDOWS)):
        lanes = slice(k * POOL_GC, POOL_WIDTH)
        lo, back = 8 * k, 2 ** (k - 1)
        lv_ref[k - 1, lo:end, lanes] = level(k - 1, lo, end, lanes) + level(k - 1, lo - back, end - back, lanes)

    y_grp = []
    for s in subs:
        r0 = POOL_HIST + s * SUB_ROWS
        pos = t * tm + s * SUB_ROWS + lax.broadcasted_iota(jnp.int32, (SUB_ROWS, POOL_GC), 0)
        ys = []
        for g, w in enumerate(POOL_WINDOWS):
            lanes = slice(g * POOL_GC, (g + 1) * POOL_GC)
            acc = level(g, r0, r0 + SUB_ROWS, lanes) + level(g, r0 - w // 2, r0 - w // 2 + SUB_ROWS, lanes)
            cnt = jnp.minimum(pos + 1, w).astype(_F32)
            ys.append(mm(acc / cnt - u[s][:, lanes], wg_ref[g]))
        y_grp.append(jnp.concatenate(ys, axis=1))
    y_pool = [mm(y_grp[s] * ps_ref[...] * jax.nn.silu(z_pool[s]), wpu_ref[...]) for s in subs]

    merged = [jax.nn.sigmoid(m_pool[s]) * y_pool[s] + jax.nn.sigmoid(m_attn[s]) * y_attn[s] for s in subs]
    half = SUB_ROWS // 2
    pieces = [(s, 0, SUB_ROWS) for s in subs[:-1]] + [(subs[-1], 0, half), (subs[-1], half, SUB_ROWS)]
    res = [mm(merged[s][r0:r1], wo_ref[...]) for s, r0, r1 in pieces]
    for (s, r0, r1), r in zip(pieces, res):
        piece_rows = slice(s * SUB_ROWS + r0, s * SUB_ROWS + r1)
        xn = x_ref[0, piece_rows, :] + mod_ref[0, 2:3, :] * r
        ms = jnp.mean(xn * xn, axis=-1, keepdims=True)
        out_ref[0, piece_rows, :] = xn * lax.rsqrt(ms + EPS) * gf_ref[...]


def _pool_merge_out(x, h, mod, g_final, w_all, w_grp, pool_scale, w_pool_up, w_attn_up, w_out, att):
    assert _ZA_COL % _TAIL_W == 0
    b, s, d = x.shape
    tm = TM_OUT
    full = lambda *shape: pl.BlockSpec(shape, lambda bi, ti: (0,) * len(shape))
    return pl.pallas_call(
        _out_kernel,
        grid=(b, s // tm),
        in_specs=[pl.BlockSpec((1, tm, d), lambda bi, ti: (bi, ti, 0)),
                  pl.BlockSpec((1, tm, d), lambda bi, ti: (bi, ti, 0)),
                  pl.BlockSpec((1, 3, d), lambda bi, ti: (bi, 0, 0)),
                  full(1, d),
                  pl.BlockSpec((d, _HEAD_W), lambda bi, ti: (0, 0)),
                  pl.BlockSpec((d, _TAIL_W), lambda bi, ti: (0, _ZA_COL // _TAIL_W)),
                  full(len(POOL_WINDOWS), POOL_GC, POOL_GC),
                  full(1, POOL_WIDTH),
                  full(POOL_WIDTH, d), full(ATTN_WIDTH, d), full(d, d),
                  pl.BlockSpec((1, tm, ATTN_WIDTH), lambda bi, ti: (bi, ti, 0))],
        out_specs=pl.BlockSpec((1, tm, d), lambda bi, ti: (bi, ti, 0)),
        out_shape=jax.ShapeDtypeStruct((b, s, d), _F32),
        scratch_shapes=[pltpu.VMEM((POOL_HIST + tm, POOL_WIDTH), _F32),
                        pltpu.VMEM((POOL_HIST, POOL_WIDTH), _F32),
                        pltpu.VMEM((len(POOL_WINDOWS) - 1, POOL_HIST + tm, POOL_WIDTH), _F32)],
        compiler_params=pltpu.CompilerParams(dimension_semantics=("arbitrary", "arbitrary"),
                                             vmem_limit_bytes=VMEM_LIMIT_BYTES),
        name="pool_merge_out",
    )(x, h, mod, g_final, w_all, w_all, w_grp, pool_scale, w_pool_up, w_attn_up, w_out, att)


def _rope_tables(s):
    pos = jnp.arange(s, dtype=_F32)
    inv_freq = ROPE_THETA ** (-jnp.arange(0, ROT_DIM, 2, dtype=_F32) / ROT_DIM)
    ang_t = inv_freq[:, None] * pos[None, :]
    dim = jnp.arange(128) % HEAD_DIM
    rot = dim < ROT_DIM
    ang_l = pos[:, None] * inv_freq[dim % ROT_HALF][None, :]
    cos_l = jnp.where(rot, jnp.cos(ang_l), 1.0)
    sin_l = jnp.where(rot, jnp.where(dim < ROT_HALF, -1.0, 1.0) * jnp.sin(ang_l), 0.0)
    return jnp.cos(ang_t), jnp.sin(ang_t), cos_l, sin_l


def kernel(x, c, w_ada, b_ada, g_norm, w_in, w_pool_grp, pool_scale, w_pool_up, w_attn_up, w_out, g_final):
    b, s, d = x.shape
    assert w_ada.shape[0] == 1, "single-layer configuration only"
    assert d == D_MODEL and s % TM_QKV == 0 and s % TM_OUT == 0
    assert w_in.shape[1:] == (d, IN_WIDTH)
    w_all = w_in[0].astype(_BF16)
    wq_t = w_all[:, _Q_COL:_K_COL].T
    wv_t = w_all[:, _V_COL:_ZA_COL].T

    mod = _adaln_mod(c, w_ada[0], b_ada[0])
    cos_t, sin_t, cos_l, sin_l = _rope_tables(s)
    g1 = g_norm[0].reshape(1, d)
    q_t, k, v_t, kmean, h = _qkv_proj(x, mod, g1, wq_t, wv_t, w_all, cos_t, sin_t, cos_l, sin_l)
    att = _moba_attention(q_t, k, v_t, kmean.reshape(b, s // MOBA_BLOCK, ATTN_WIDTH))
    return _pool_merge_out(x, h, mod, g_final.reshape(1, d), w_all,
                           w_pool_grp[0].astype(_BF16), pool_scale[0].reshape(1, POOL_WIDTH),
                           w_pool_up[0].astype(_BF16), w_attn_up[0].astype(_BF16), w_out[0].astype(_BF16), att)
```

```python
import functools

import jax
import jax.numpy as jnp
from jax import lax
from jax.experimental import pallas as pl
from jax.experimental.pallas import tpu as pltpu

D_MODEL = 1024
POOL_WIDTH = 512
POOL_WINDOWS = (2, 4, 8, 16)
POOL_GC = 128
POOL_HIST = 8 * len(POOL_WINDOWS)
assert all(w == 2 ** (g + 1) for g, w in enumerate(POOL_WINDOWS))
HEAD_DIM = 64
N_HEADS = 8
ATTN_WIDTH = N_HEADS * HEAD_DIM
MOBA_BLOCK = 256
MOBA_TOPK = 3
ROT_DIM = 16
ROT_HALF = ROT_DIM // 2
ROPE_THETA = 500000.0
EPS = 1e-6

TM_QKV = 1024
TM_OUT = 512
VMEM_LIMIT_BYTES = 56 * 1024 * 1024

_Q_COL, _K_COL, _V_COL, _ZA_COL = 1024, 1536, 2048, 2560
IN_WIDTH = 5120

_BF16 = jnp.bfloat16
_F32 = jnp.float32
_NT_DIMS = (((1,), (1,)), ((), ()))


def _adaln_norm(x, g, mod_ref):
    ms = jnp.mean(x * x, axis=-1, keepdims=True)
    shift = mod_ref[0, 0:1, :]
    scl = mod_ref[0, 1:2, :]
    return x * lax.rsqrt(ms + EPS) * (g * (1.0 + scl)) + shift


def _mod_kernel(c_ref, w_ref, b_ref, o_ref):
    o_ref[...] = jnp.dot(c_ref[...].astype(_BF16), w_ref[...].astype(_BF16),
                         preferred_element_type=_F32) + b_ref[...]


def _adaln_mod(c, w_ada, b_ada):
    nb_, d = c.shape
    rows = -(-nb_ // 8) * 8
    c_pad = jnp.zeros((rows, d), _F32).at[:nb_].set(c)
    n_out = w_ada.shape[1]
    out = pl.pallas_call(
        _mod_kernel,
        grid=(n_out // d,),
        in_specs=[pl.BlockSpec((rows, d), lambda j: (0, 0)),
                  pl.BlockSpec((d, d), lambda j: (0, j)),
                  pl.BlockSpec((1, d), lambda j: (0, j))],
        out_specs=pl.BlockSpec((rows, d), lambda j: (0, j)),
        out_shape=jax.ShapeDtypeStruct((rows, n_out), _F32),
        name="adaln_mod",
    )(c_pad, w_ada, b_ada.reshape(1, n_out))
    return out[:nb_].reshape(nb_, n_out // d, d)


def _qkv_kernel(x_ref, mod_ref, g_ref, wq_ref, wv_ref, wk_ref, cos_t_ref, sin_t_ref, cos_l_ref, sin_l_ref,
                qt_ref, k_ref, vt_ref, km_ref, h_ref):
    blocks = range(x_ref.shape[1] // MOBA_BLOCK)
    rows = [slice(i * MOBA_BLOCK, (i + 1) * MOBA_BLOCK) for i in blocks]
    h = [_adaln_norm(x_ref[0, rows[i], :], g_ref[...], mod_ref).astype(_BF16) for i in blocks]

    q_t, v_t, k = [], [], []
    for i in blocks:
        k.append(jnp.dot(h[i], wk_ref[...], preferred_element_type=_F32))
        q_t.append(lax.dot_general(wq_ref[...], h[i], _NT_DIMS, preferred_element_type=_F32))
        v_t.append(lax.dot_general(wv_ref[...], h[i], _NT_DIMS, preferred_element_type=_F32))

    dim = lax.broadcasted_iota(jnp.int32, (MOBA_BLOCK, ATTN_WIDTH), 1) % HEAD_DIM
    for i in blocks:
        h_ref[0, rows[i], :] = h[i]
        cos_t = cos_t_ref[:, rows[i]]
        sin_t = sin_t_ref[:, rows[i]]
        pieces = []
        for hd in range(N_HEADS):
            base = hd * HEAD_DIM
            x1 = q_t[i][base:base + ROT_HALF]
            x2 = q_t[i][base + ROT_HALF:base + ROT_DIM]
            pieces += [x1 * cos_t - x2 * sin_t, x2 * cos_t + x1 * sin_t,
                       q_t[i][base + ROT_DIM:base + HEAD_DIM]]
        qt_ref[0, i] = (jnp.concatenate(pieces, axis=0) * (HEAD_DIM ** -0.5 * LOG2_E)).astype(_BF16)
        vt_ref[0, i] = v_t[i].astype(_BF16)

        cos_l = jnp.concatenate([cos_l_ref[rows[i], :]] * (ATTN_WIDTH // 128), axis=1)
        sin_l = jnp.concatenate([sin_l_ref[rows[i], :]] * (ATTN_WIDTH // 128), axis=1)
        partner = jnp.where(dim < ROT_HALF,
                            pltpu.roll(k[i], ATTN_WIDTH - ROT_HALF, axis=1),
                            pltpu.roll(k[i], ROT_HALF, axis=1))
        kb = jnp.where(dim < ROT_DIM, k[i] * cos_l + partner * sin_l, k[i])
        km_ref[0, i] = jnp.mean(kb, axis=0, keepdims=True)
        k_ref[0, i] = kb.astype(_BF16)


def _qkv_proj(x, mod, g_norm, wq_t, wv_t, w_all, cos_t, sin_t, cos_l, sin_l):
    b, s, d = x.shape
    nb = s // MOBA_BLOCK
    tm = TM_QKV
    nbt = tm // MOBA_BLOCK
    full = lambda *shape: pl.BlockSpec(shape, lambda bi, ti: (0,) * len(shape))
    return pl.pallas_call(
        _qkv_kernel,
        grid=(b, s // tm),
        in_specs=[pl.BlockSpec((1, tm, d), lambda bi, ti: (bi, ti, 0)),
                  pl.BlockSpec((1, 3, d), lambda bi, ti: (bi, 0, 0)),
                  full(1, d),
                  full(ATTN_WIDTH, d),
                  full(ATTN_WIDTH, d),
                  pl.BlockSpec((d, ATTN_WIDTH), lambda bi, ti: (0, _K_COL // ATTN_WIDTH)),
                  pl.BlockSpec((ROT_HALF, tm), lambda bi, ti: (0, ti)),
                  pl.BlockSpec((ROT_HALF, tm), lambda bi, ti: (0, ti)),
                  pl.BlockSpec((tm, 128), lambda bi, ti: (ti, 0)),
                  pl.BlockSpec((tm, 128), lambda bi, ti: (ti, 0))],
        out_specs=[pl.BlockSpec((1, nbt, ATTN_WIDTH, MOBA_BLOCK), lambda bi, ti: (bi, ti, 0, 0)),
                   pl.BlockSpec((1, nbt, MOBA_BLOCK, ATTN_WIDTH), lambda bi, ti: (bi, ti, 0, 0)),
                   pl.BlockSpec((1, nbt, ATTN_WIDTH, MOBA_BLOCK), lambda bi, ti: (bi, ti, 0, 0)),
                   pl.BlockSpec((1, nbt, 1, ATTN_WIDTH), lambda bi, ti: (bi, ti, 0, 0)),
                   pl.BlockSpec((1, tm, d), lambda bi, ti: (bi, ti, 0))],
        out_shape=[jax.ShapeDtypeStruct((b, nb, ATTN_WIDTH, MOBA_BLOCK), _BF16),
                   jax.ShapeDtypeStruct((b, nb, MOBA_BLOCK, ATTN_WIDTH), _BF16),
                   jax.ShapeDtypeStruct((b, nb, ATTN_WIDTH, MOBA_BLOCK), _BF16),
                   jax.ShapeDtypeStruct((b, nb, 1, ATTN_WIDTH), _F32),
                   jax.ShapeDtypeStruct((b, s, d), _BF16)],
        compiler_params=pltpu.CompilerParams(dimension_semantics=("arbitrary", "arbitrary"),
                                             vmem_limit_bytes=VMEM_LIMIT_BYTES),
        name="qkv_proj",
    )(x, mod, g_norm, wq_t, wv_t, w_all, cos_t, sin_t, cos_l, sin_l)


V_EXT = HEAD_DIM + 16
QK_AHEAD = 4
PV_BEHIND = 1
Q_SUB = 2
N_LANES = Q_SUB * N_HEADS
KV_TRIPS = (8, 4, 2)
LOG2_E = 1.4426950408889634


def _attn_kernel(qt_ref, k_ref, vt_ref, km_ref, o_ref, qh_ref, sel_ref, acc_ref, s_ref):
    i0 = pl.program_id(1) * Q_SUB
    nb = k_ref.shape[1]
    blk = MOBA_BLOCK
    neg_inf = -jnp.inf
    lanes = [(qs, hd) for qs in range(Q_SUB) for hd in range(N_HEADS)]

    km = km_ref[0].astype(_BF16)
    k_own = [k_ref[0, i0 + qs] for qs in range(Q_SUB)]
    v_own = [vt_ref[0, i0 + qs] for qs in range(Q_SUB)]
    pair_row = lax.broadcasted_iota(jnp.int32, (2 * HEAD_DIM, blk), 0)
    blk_id = lax.broadcasted_iota(jnp.int32, (nb, blk), 0)
    key_pos = lax.broadcasted_iota(jnp.int32, (blk, blk), 0)
    qry_pos = lax.broadcasted_iota(jnp.int32, (blk, blk), 1)
    ones = jnp.ones((V_EXT - HEAD_DIM, blk), _BF16)

    def pair_cols(arr, hd):
        p0 = (hd // 2) * 2 * HEAD_DIM
        return arr[:, p0:p0 + 2 * HEAD_DIM]

    def v_ext(v_blk, hd):
        return jnp.concatenate([v_blk[hd * HEAD_DIM:(hd + 1) * HEAD_DIM], ones], axis=0)

    def software_pipeline(items, score, softmax, accumulate):
        n_items = len(items)
        scores, probs = {}, {}
        for step in range(-QK_AHEAD, n_items + PV_BEHIND):
            if 0 <= step + QK_AHEAD < n_items:
                scores[step + QK_AHEAD] = score(items[step + QK_AHEAD])
            if 0 <= step < n_items:
                probs[step] = softmax(items[step], scores.pop(step))
            if 0 <= step - PV_BEHIND < n_items:
                accumulate(items[step - PV_BEHIND], probs.pop(step - PV_BEHIND))

    for ln, (qs, hd) in enumerate(lanes):
        p0 = (hd // 2) * 2 * HEAD_DIM
        in_head = (pair_row >= HEAD_DIM) == (hd % 2 == 1)
        qh_ref[ln] = jnp.where(in_head, qt_ref[0, qs, p0:p0 + 2 * HEAD_DIM, :], jnp.zeros((), _BF16))

    def gate_scores(ln):
        return jnp.dot(pair_cols(km, lanes[ln][1]), qh_ref[ln], preferred_element_type=_F32)

    def select_blocks(ln, gate):
        past = blk_id < i0 + lanes[ln][0]
        gate = jnp.where(past, gate, neg_inf)
        sel = jnp.zeros((nb, blk), _F32)
        for _ in range(MOBA_TOPK):
            best = jnp.max(gate, axis=0, keepdims=True)
            idx = jnp.min(jnp.where(gate == best, blk_id, nb), axis=0, keepdims=True)
            pick = blk_id == idx
            sel = jnp.where(pick, 1.0, sel)
            gate = jnp.where(pick, neg_inf, gate)
        sel_ref[ln] = jnp.where(past, sel, 0.0)

    software_pipeline(list(range(N_LANES)), gate_scores, select_blocks, lambda ln, _: None)

    def qk(k_blk, ln):
        return jnp.dot(pair_cols(k_blk, lanes[ln][1]), qh_ref[ln], preferred_element_type=_F32)

    def past_softmax(ln, j, s, m_cur):
        chosen = sel_ref[ln, pl.ds(j, 1), :] > 0.5
        m_old = m_cur[ln]
        m_new = jnp.where(chosen, jnp.maximum(m_old, jnp.max(s, axis=0, keepdims=True)), m_old)
        m_cur[ln] = m_new
        shift = jnp.where(chosen, m_new, jnp.inf)
        return jnp.exp2(s - shift).astype(_BF16), jnp.exp2(m_old - m_new)

    def accumulate(ln, v_blk, p, alpha):
        acc_ref[ln] = alpha * acc_ref[ln] + jnp.dot(v_ext(v_blk, lanes[ln][1]), p, preferred_element_type=_F32)

    handover = [("next", n) for n in range(QK_AHEAD)]

    m_cur = [None] * N_LANES
    own_items = [("own", ln, qs) for ln, (qs, _) in enumerate(lanes)]
    own_items += [("past", ln, jb) for jb in range(Q_SUB) for ln, (qs, _) in enumerate(lanes) if jb < qs]

    def own_score(item):
        kind, ln = item[:2]
        return qk(k_ref[0, 0], ln) if kind == "next" else qk(k_own[item[2]], ln)

    def own_softmax(item, s):
        kind, ln = item[:2]
        if kind == "next":
            s_ref[ln] = s
            return None
        if kind == "past":
            return past_softmax(ln, i0 + item[2], s, m_cur)
        s = jnp.where(key_pos <= qry_pos, s, neg_inf)
        m_cur[ln] = jnp.max(s, axis=0, keepdims=True)
        return jnp.exp2(s - m_cur[ln]).astype(_BF16), None

    def own_accumulate(item, p_alpha):
        kind, ln = item[:2]
        if kind == "own":
            acc_ref[ln] = jnp.dot(v_ext(v_own[item[2]], lanes[ln][1]), p_alpha[0], preferred_element_type=_F32)
        elif kind == "past":
            accumulate(ln, v_own[item[2]], *p_alpha)

    software_pipeline(own_items + handover, own_score, own_softmax, own_accumulate)

    def tile_of(n):
        return divmod(n, N_LANES)

    def past_blocks(n_blocks, first, t, m_run):
        j0 = first + t * n_blocks
        j_next = jnp.minimum(j0 + n_blocks, nb - 1)
        k_j = [k_ref[0, j0 + u] for u in range(n_blocks)]
        v_j = [vt_ref[0, j0 + u] for u in range(n_blocks)]
        m_trip = list(m_run)

        def score(item):
            kind, n = item
            u, ln = tile_of(n)
            if kind == "next":
                return qk(k_ref[0, j_next], n)
            return s_ref[n] if n < QK_AHEAD else qk(k_j[u], ln)

        def softmax(item, s):
            kind, n = item
            if kind == "next":
                s_ref[n] = s
                return None
            u, ln = tile_of(n)
            return past_softmax(ln, j0 + u, s, m_trip)

        def finish(item, p_alpha):
            kind, n = item
            if kind == "tile":
                u, ln = tile_of(n)
                accumulate(ln, v_j[u], *p_alpha)

        software_pipeline([("tile", n) for n in range(n_blocks * N_LANES)] + handover, score, softmax, finish)
        return tuple(m_trip)

    m_run, done = tuple(m_cur), 0
    for n_blocks in KV_TRIPS:
        trips = (i0 - done) // n_blocks
        m_run = lax.fori_loop(0, trips, functools.partial(past_blocks, n_blocks, done), m_run)
        done = done + trips * n_blocks

    for qs in range(Q_SUB):
        outs = []
        for ln in range(qs * N_HEADS, (qs + 1) * N_HEADS):
            acc = acc_ref[ln]
            outs.append(acc[:HEAD_DIM] / acc[HEAD_DIM:HEAD_DIM + 1])
        o_ref[0, qs * blk:(qs + 1) * blk, :] = jnp.concatenate(outs, axis=0).T.astype(o_ref.dtype)


def _moba_attention(q_t, k, v_t, kmean):
    b, nb = k.shape[0], k.shape[1]
    s = nb * MOBA_BLOCK
    assert nb % Q_SUB == 0 and Q_SUB % KV_TRIPS[-1] == 0 and QK_AHEAD <= N_HEADS
    return pl.pallas_call(
        _attn_kernel,
        grid=(b, nb // Q_SUB),
        in_specs=[pl.BlockSpec((1, Q_SUB, ATTN_WIDTH, MOBA_BLOCK), lambda bi, qi: (bi, qi, 0, 0)),
                  pl.BlockSpec((1, nb, MOBA_BLOCK, ATTN_WIDTH), lambda bi, qi: (bi, 0, 0, 0)),
                  pl.BlockSpec((1, nb, ATTN_WIDTH, MOBA_BLOCK), lambda bi, qi: (bi, 0, 0, 0)),
                  pl.BlockSpec((1, nb, ATTN_WIDTH), lambda bi, qi: (bi, 0, 0))],
        out_specs=pl.BlockSpec((1, Q_SUB * MOBA_BLOCK, ATTN_WIDTH), lambda bi, qi: (bi, qi, 0)),
        out_shape=jax.ShapeDtypeStruct((b, s, ATTN_WIDTH), _BF16),
        scratch_shapes=[pltpu.VMEM((N_LANES, 2 * HEAD_DIM, MOBA_BLOCK), _BF16),
                        pltpu.VMEM((N_LANES, nb, MOBA_BLOCK), _F32),
                        pltpu.VMEM((N_LANES, V_EXT, MOBA_BLOCK), _F32),
                        pltpu.VMEM((QK_AHEAD, MOBA_BLOCK, MOBA_BLOCK), _F32)],
        compiler_params=pltpu.CompilerParams(dimension_semantics=("arbitrary", "arbitrary"),
                                             vmem_limit_bytes=VMEM_LIMIT_BYTES),
        name="moba_attention",
    )(q_t, k, v_t, kmean)


_HEAD_W = _Q_COL
_TAIL_W = IN_WIDTH - _ZA_COL
SUB_ROWS = 256


def _out_kernel(x_ref, h_ref, mod_ref, gf_ref, wh_ref, wt_ref, wg_ref, ps_ref, wpu_ref, wau_ref, wo_ref, att_ref,
                out_ref, hist_ref, carry_ref, lv_ref):
    t = pl.program_id(1)
    tm = x_ref.shape[1]
    subs = range(tm // SUB_ROWS)
    rows = [slice(s * SUB_ROWS, (s + 1) * SUB_ROWS) for s in subs]

    def mm(a, w):
        return jnp.dot(a.astype(_BF16), w, preferred_element_type=_F32)

    @pl.when(t == 0)
    def _():
        carry_ref[...] = jnp.zeros_like(carry_ref)

    h = [h_ref[0, r, :] for r in rows]
    u = [mm(h[s], wh_ref[:, :POOL_WIDTH]) for s in subs]
    z_attn = [mm(h[s], wt_ref[:, :ATTN_WIDTH]) for s in subs]
    z_pool = [mm(h[s], wh_ref[:, POOL_WIDTH:]) for s in subs]
    m_attn = [mm(h[s], wt_ref[:, ATTN_WIDTH + D_MODEL:]) for s in subs]
    m_pool = [mm(h[s], wt_ref[:, ATTN_WIDTH:ATTN_WIDTH + D_MODEL]) for s in subs]

    hist_ref[0:POOL_HIST, :] = carry_ref[...]
    for s in subs:
        hist_ref[POOL_HIST + s * SUB_ROWS:POOL_HIST + (s + 1) * SUB_ROWS, :] = u[s]
    carry_ref[...] = u[-1][SUB_ROWS - POOL_HIST:, :]

    y_attn = [mm(att_ref[0, rows[s], :].astype(_F32) * jax.nn.silu(z_attn[s]), wau_ref[...]) for s in subs]

    end = POOL_HIST + tm

    def level(k, r0, r1, lanes):
        return hist_ref[r0:r1, lanes] if k == 0 else lv_ref[k - 1, r0:r1, lanes]

    for k in range(1, len(POOL_WINDOWS)):
        lanes = slice(k * POOL_GC, POOL_WIDTH)
        lo, back = 8 * k, 2 ** (k - 1)
        lv_ref[k - 1, lo:end, lanes] = level(k - 1, lo, end, lanes) + level(k - 1, lo - back, end - back, lanes)

    y_grp = []
    for s in subs:
        r0 = POOL_HIST + s * SUB_ROWS
        pos = t * tm + s * SUB_ROWS + lax.broadcasted_iota(jnp.int32, (SUB_ROWS, POOL_GC), 0)
        ys = []
        for g, w in enumerate(POOL_WINDOWS):
            lanes = slice(g * POOL_GC, (g + 1) * POOL_GC)
            acc = level(g, r0, r0 + SUB_ROWS, lanes) + level(g, r0 - w // 2, r0 - w // 2 + SUB_ROWS, lanes)
            cnt = jnp.minimum(pos + 1, w).astype(_F32)
            ys.append(mm(acc / cnt - u[s][:, lanes], wg_ref[g]))
        y_grp.append(jnp.concatenate(ys, axis=1))
    y_pool = [mm(y_grp[s] * ps_ref[...] * jax.nn.silu(z_pool[s]), wpu_ref[...]) for s in subs]

    merged = [jax.nn.sigmoid(m_pool[s]) * y_pool[s] + jax.nn.sigmoid(m_attn[s]) * y_attn[s] for s in subs]
    half = SUB_ROWS // 2
    pieces = [(s, 0, SUB_ROWS) for s in subs[:-1]] + [(subs[-1], 0, half), (subs[-1], half, SUB_ROWS)]
    res = [mm(merged[s][r0:r1], wo_ref[...]) for s, r0, r1 in pieces]
    for (s, r0, r1), r in zip(pieces, res):
        piece_rows = slice(s * SUB_ROWS + r0, s * SUB_ROWS + r1)
        xn = x_ref[0, piece_rows, :] + mod_ref[0, 2:3, :] * r
        ms = jnp.mean(xn * xn, axis=-1, keepdims=True)
        out_ref[0, piece_rows, :] = xn * lax.rsqrt(ms + EPS) * gf_ref[...]


def _pool_merge_out(x, h, mod, g_final, w_all, w_grp, pool_scale, w_pool_up, w_attn_up, w_out, att):
    assert _ZA_COL % _TAIL_W == 0
    b, s, d = x.shape
    tm = TM_OUT
    full = lambda *shape: pl.BlockSpec(shape, lambda bi, ti: (0,) * len(shape))
    return pl.pallas_call(
        _out_kernel,
        grid=(b, s // tm),
        in_specs=[pl.BlockSpec((1, tm, d), lambda bi, ti: (bi, ti, 0)),
                  pl.BlockSpec((1, tm, d), lambda bi, ti: (bi, ti, 0)),
                  pl.BlockSpec((1, 3, d), lambda bi, ti: (bi, 0, 0)),
                  full(1, d),
                  pl.BlockSpec((d, _HEAD_W), lambda bi, ti: (0, 0)),
                  pl.BlockSpec((d, _TAIL_W), lambda bi, ti: (0, _ZA_COL // _TAIL_W)),
                  full(len(POOL_WINDOWS), POOL_GC, POOL_GC),
                  full(1, POOL_WIDTH),
                  full(POOL_WIDTH, d), full(ATTN_WIDTH, d), full(d, d),
                  pl.BlockSpec((1, tm, ATTN_WIDTH), lambda bi, ti: (bi, ti, 0))],
        out_specs=pl.BlockSpec((1, tm, d), lambda bi, ti: (bi, ti, 0)),
        out_shape=jax.ShapeDtypeStruct((b, s, d), _F32),
        scratch_shapes=[pltpu.VMEM((POOL_HIST + tm, POOL_WIDTH), _F32),
                        pltpu.VMEM((POOL_HIST, POOL_WIDTH), _F32),
                        pltpu.VMEM((len(POOL_WINDOWS) - 1, POOL_HIST + tm, POOL_WIDTH), _F32)],
        compiler_params=pltpu.CompilerParams(dimension_semantics=("arbitrary", "arbitrary"),
                                             vmem_limit_bytes=VMEM_LIMIT_BYTES),
        name="pool_merge_out",
    )(x, h, mod, g_final, w_all, w_all, w_grp, pool_scale, w_pool_up, w_attn_up, w_out, att)


def _rope_tables(s):
    pos = jnp.arange(s, dtype=_F32)
    inv_freq = ROPE_THETA ** (-jnp.arange(0, ROT_DIM, 2, dtype=_F32) / ROT_DIM)
    ang_t = inv_freq[:, None] * pos[None, :]
    dim = jnp.arange(128) % HEAD_DIM
    rot = dim < ROT_DIM
    ang_l = pos[:, None] * inv_freq[dim % ROT_HALF][None, :]
    cos_l = jnp.where(rot, jnp.cos(ang_l), 1.0)
    sin_l = jnp.where(rot, jnp.where(dim < ROT_HALF, -1.0, 1.0) * jnp.sin(ang_l), 0.0)
    return jnp.cos(ang_t), jnp.sin(ang_t), cos_l, sin_l


def kernel(x, c, w_ada, b_ada, g_norm, w_in, w_pool_grp, pool_scale, w_pool_up, w_attn_up, w_out, g_final):
    b, s, d = x.shape
    assert w_ada.shape[0] == 1, "single-layer configuration only"
    assert d == D_MODEL and s % TM_QKV == 0 and s % TM_OUT == 0
    assert w_in.shape[1:] == (d, IN_WIDTH)
    w_all = w_in[0].astype(_BF16)
    wq_t = w_all[:, _Q_COL:_K_COL].T
    wv_t = w_all[:, _V_COL:_ZA_COL].T

    mod = _adaln_mod(c, w_ada[0], b_ada[0])
    cos_t, sin_t, cos_l, sin_l = _rope_tables(s)
    g1 = g_norm[0].reshape(1, d)
    q_t, k, v_t, kmean, h = _qkv_proj(x, mod, g1, wq_t, wv_t, w_all, cos_t, sin_t, cos_l, sin_l)
    att = _moba_attention(q_t, k, v_t, kmean.reshape(b, s // MOBA_BLOCK, ATTN_WIDTH))
    return _pool_merge_out(x, h, mod, g_final.reshape(1, d), w_all,
                           w_pool_grp[0].astype(_BF16), pool_scale[0].reshape(1, POOL_WIDTH),
                           w_pool_up[0].astype(_BF16), w_attn_up[0].astype(_BF16), w_out[0].astype(_BF16), att)
```

```python
import functools

import jax
import jax.numpy as jnp
from jax import lax
from jax.experimental import pallas as pl
from jax.experimental.pallas import tpu as pltpu

SUBLANES = 8
LANES = 128
BF16_ROWS = 16

D_MODEL = 1024
POOL_WIDTH = 512
POOL_WINDOWS = (2, 4, 8, 16)
POOL_GC = 128
POOL_HIST = SUBLANES * len(POOL_WINDOWS)
assert all(w == 2 ** (g + 1) for g, w in enumerate(POOL_WINDOWS))
HEAD_DIM = 64
N_HEADS = 8
ATTN_WIDTH = N_HEADS * HEAD_DIM
MOBA_BLOCK = 256
MOBA_TOPK = 3
ROT_DIM = 16
ROT_HALF = ROT_DIM // 2
ROPE_THETA = 500000.0
EPS = 1e-6

TM_QKV = 1024
TM_OUT = 512
VMEM_LIMIT_BYTES = 56 * 1024 * 1024

_Q_COL, _K_COL, _V_COL, _ZA_COL = 1024, 1536, 2048, 2560
IN_WIDTH = 5120

_BF16 = jnp.bfloat16
_F32 = jnp.float32
_NT_DIMS = (((1,), (1,)), ((), ()))


def _adaln_norm(x, g, mod_ref):
    ms = jnp.mean(x * x, axis=-1, keepdims=True)
    shift = mod_ref[0, 0:1, :]
    scl = mod_ref[0, 1:2, :]
    return x * lax.rsqrt(ms + EPS) * (g * (1.0 + scl)) + shift


def _mod_kernel(c_ref, w_ref, b_ref, o_ref):
    o_ref[...] = jnp.dot(c_ref[...].astype(_BF16), w_ref[...].astype(_BF16),
                         preferred_element_type=_F32) + b_ref[...]


def _adaln_mod(c, w_ada, b_ada):
    nb_, d = c.shape
    rows = -(-nb_ // SUBLANES) * SUBLANES
    c_pad = jnp.zeros((rows, d), _F32).at[:nb_].set(c)
    n_out = w_ada.shape[1]
    out = pl.pallas_call(
        _mod_kernel,
        grid=(n_out // d,),
        in_specs=[pl.BlockSpec((rows, d), lambda j: (0, 0)),
                  pl.BlockSpec((d, d), lambda j: (0, j)),
                  pl.BlockSpec((1, d), lambda j: (0, j))],
        out_specs=pl.BlockSpec((rows, d), lambda j: (0, j)),
        out_shape=jax.ShapeDtypeStruct((rows, n_out), _F32),
        name="adaln_mod",
    )(c_pad, w_ada, b_ada.reshape(1, n_out))
    return out[:nb_].reshape(nb_, n_out // d, d)


def _qkv_kernel(x_ref, mod_ref, g_ref, wq_ref, wv_ref, wk_ref, cos_t_ref, sin_t_ref, cos_l_ref, sin_l_ref,
                qt_ref, k_ref, vt_ref, km_ref, h_ref):
    blocks = range(x_ref.shape[1] // MOBA_BLOCK)
    rows = [slice(i * MOBA_BLOCK, (i + 1) * MOBA_BLOCK) for i in blocks]
    h = [_adaln_norm(x_ref[0, rows[i], :], g_ref[...], mod_ref).astype(_BF16) for i in blocks]

    q_t, v_t, k = [], [], []
    for i in blocks:
        k.append(jnp.dot(h[i], wk_ref[...], preferred_element_type=_F32))
        q_t.append(lax.dot_general(wq_ref[...], h[i], _NT_DIMS, preferred_element_type=_F32))
        v_t.append(lax.dot_general(wv_ref[...], h[i], _NT_DIMS, preferred_element_type=_F32))

    dim = lax.broadcasted_iota(jnp.int32, (MOBA_BLOCK, ATTN_WIDTH), 1) % HEAD_DIM
    for i in blocks:
        h_ref[0, rows[i], :] = h[i]
        cos_t = cos_t_ref[:, rows[i]]
        sin_t = sin_t_ref[:, rows[i]]
        pieces = []
        for hd in range(N_HEADS):
            base = hd * HEAD_DIM
            x1 = q_t[i][base:base + ROT_HALF]
            x2 = q_t[i][base + ROT_HALF:base + ROT_DIM]
            pieces += [x1 * cos_t - x2 * sin_t, x2 * cos_t + x1 * sin_t,
                       q_t[i][base + ROT_DIM:base + HEAD_DIM]]
        qt_ref[0, i] = (jnp.concatenate(pieces, axis=0) * (HEAD_DIM ** -0.5 * LOG2_E)).astype(_BF16)
        vt_ref[0, i] = v_t[i].astype(_BF16)

        cos_l = jnp.concatenate([cos_l_ref[rows[i], :]] * (ATTN_WIDTH // LANES), axis=1)
        sin_l = jnp.concatenate([sin_l_ref[rows[i], :]] * (ATTN_WIDTH // LANES), axis=1)
        partner = jnp.where(dim < ROT_HALF,
                            pltpu.roll(k[i], ATTN_WIDTH - ROT_HALF, axis=1),
                            pltpu.roll(k[i], ROT_HALF, axis=1))
        kb = jnp.where(dim < ROT_DIM, k[i] * cos_l + partner * sin_l, k[i])
        km_ref[0, i] = jnp.mean(kb, axis=0, keepdims=True)
        k_ref[0, i] = kb.astype(_BF16)


def _qkv_proj(x, mod, g_norm, wq_t, wv_t, w_all, cos_t, sin_t, cos_l, sin_l):
    b, s, d = x.shape
    nb = s // MOBA_BLOCK
    tm = TM_QKV
    nbt = tm // MOBA_BLOCK
    full = lambda *shape: pl.BlockSpec(shape, lambda bi, ti: (0,) * len(shape))
    return pl.pallas_call(
        _qkv_kernel,
        grid=(b, s // tm),
        in_specs=[pl.BlockSpec((1, tm, d), lambda bi, ti: (bi, ti, 0)),
                  pl.BlockSpec((1, 3, d), lambda bi, ti: (bi, 0, 0)),
                  full(1, d),
                  full(ATTN_WIDTH, d),
                  full(ATTN_WIDTH, d),
                  pl.BlockSpec((d, ATTN_WIDTH), lambda bi, ti: (0, _K_COL // ATTN_WIDTH)),
                  pl.BlockSpec((ROT_HALF, tm), lambda bi, ti: (0, ti)),
                  pl.BlockSpec((ROT_HALF, tm), lambda bi, ti: (0, ti)),
                  pl.BlockSpec((tm, LANES), lambda bi, ti: (ti, 0)),
                  pl.BlockSpec((tm, LANES), lambda bi, ti: (ti, 0))],
        out_specs=[pl.BlockSpec((1, nbt, ATTN_WIDTH, MOBA_BLOCK), lambda bi, ti: (bi, ti, 0, 0)),
                   pl.BlockSpec((1, nbt, MOBA_BLOCK, ATTN_WIDTH), lambda bi, ti: (bi, ti, 0, 0)),
                   pl.BlockSpec((1, nbt, ATTN_WIDTH, MOBA_BLOCK), lambda bi, ti: (bi, ti, 0, 0)),
                   pl.BlockSpec((1, nbt, 1, ATTN_WIDTH), lambda bi, ti: (bi, ti, 0, 0)),
                   pl.BlockSpec((1, tm, d), lambda bi, ti: (bi, ti, 0))],
        out_shape=[jax.ShapeDtypeStruct((b, nb, ATTN_WIDTH, MOBA_BLOCK), _BF16),
                   jax.ShapeDtypeStruct((b, nb, MOBA_BLOCK, ATTN_WIDTH), _BF16),
                   jax.ShapeDtypeStruct((b, nb, ATTN_WIDTH, MOBA_BLOCK), _BF16),
                   jax.ShapeDtypeStruct((b, nb, 1, ATTN_WIDTH), _F32),
                   jax.ShapeDtypeStruct((b, s, d), _BF16)],
        compiler_params=pltpu.CompilerParams(dimension_semantics=("arbitrary", "arbitrary"),
                                             vmem_limit_bytes=VMEM_LIMIT_BYTES),
        name="qkv_proj",
    )(x, mod, g_norm, wq_t, wv_t, w_all, cos_t, sin_t, cos_l, sin_l)


V_EXT = HEAD_DIM + BF16_ROWS
QK_AHEAD = 4
PV_BEHIND = 1
Q_SUB = 4
N_LANES = Q_SUB * N_HEADS
KV_TRIPS = (4,)
LOG2_E = 1.4426950408889634


def _attn_kernel(qt_ref, k_ref, vt_ref, km_ref, o_ref, qh_ref, sel_ref, acc_ref, s_ref):
    i0 = pl.program_id(1) * Q_SUB
    nb = k_ref.shape[1]
    blk = MOBA_BLOCK
    neg_inf = -jnp.inf
    lanes = [(qs, hd) for qs in range(Q_SUB) for hd in range(N_HEADS)]

    km = km_ref[0].astype(_BF16)
    k_own = [k_ref[0, i0 + qs] for qs in range(Q_SUB)]
    v_own = [vt_ref[0, i0 + qs] for qs in range(Q_SUB)]
    pair_row = lax.broadcasted_iota(jnp.int32, (2 * HEAD_DIM, blk), 0)
    blk_id = lax.broadcasted_iota(jnp.int32, (nb, blk), 0)
    key_pos = lax.broadcasted_iota(jnp.int32, (blk, blk), 0)
    qry_pos = lax.broadcasted_iota(jnp.int32, (blk, blk), 1)
    ones = jnp.ones((V_EXT - HEAD_DIM, blk), _BF16)

    def pair_cols(arr, hd):
        p0 = (hd // 2) * 2 * HEAD_DIM
        return arr[:, p0:p0 + 2 * HEAD_DIM]

    def v_ext(v_blk, hd):
        return jnp.concatenate([v_blk[hd * HEAD_DIM:(hd + 1) * HEAD_DIM], ones], axis=0)

    def software_pipeline(items, score, softmax, accumulate):
        n_items = len(items)
        scores, probs = {}, {}
        for step in range(-QK_AHEAD, n_items + PV_BEHIND):
            if 0 <= step + QK_AHEAD < n_items:
                scores[step + QK_AHEAD] = score(items[step + QK_AHEAD])
            if 0 <= step < n_items:
                probs[step] = softmax(items[step], scores.pop(step))
            if 0 <= step - PV_BEHIND < n_items:
                accumulate(items[step - PV_BEHIND], probs.pop(step - PV_BEHIND))

    for ln, (qs, hd) in enumerate(lanes):
        p0 = (hd // 2) * 2 * HEAD_DIM
        in_head = (pair_row >= HEAD_DIM) == (hd % 2 == 1)
        qh_ref[ln] = jnp.where(in_head, qt_ref[0, qs, p0:p0 + 2 * HEAD_DIM, :], jnp.zeros((), _BF16))

    def gate_scores(ln):
        return jnp.dot(pair_cols(km, lanes[ln][1]), qh_ref[ln], preferred_element_type=_F32)

    def select_blocks(ln, gate):
        past = blk_id < i0 + lanes[ln][0]
        gate = jnp.where(past, gate, neg_inf)
        sel = jnp.zeros((nb, blk), _F32)
        for _ in range(MOBA_TOPK):
            best = jnp.max(gate, axis=0, keepdims=True)
            idx = jnp.min(jnp.where(gate == best, blk_id, nb), axis=0, keepdims=True)
            pick = blk_id == idx
            sel = jnp.where(pick, 1.0, sel)
            gate = jnp.where(pick, neg_inf, gate)
        sel_ref[ln] = jnp.where(past, sel, 0.0)

    software_pipeline(list(range(N_LANES)), gate_scores, select_blocks, lambda ln, _: None)

    def qk(k_blk, ln):
        return jnp.dot(pair_cols(k_blk, lanes[ln][1]), qh_ref[ln], preferred_element_type=_F32)

    def past_softmax(ln, j, s, m_cur):
        chosen = sel_ref[ln, pl.ds(j, 1), :] > 0.5
        m_old = m_cur[ln]
        m_new = jnp.where(chosen, jnp.maximum(m_old, jnp.max(s, axis=0, keepdims=True)), m_old)
        m_cur[ln] = m_new
        shift = jnp.where(chosen, m_new, jnp.inf)
        return jnp.exp2(s - shift).astype(_BF16), jnp.exp2(m_old - m_new)

    def accumulate(ln, v_blk, p, alpha):
        acc_ref[ln] = alpha * acc_ref[ln] + jnp.dot(v_ext(v_blk, lanes[ln][1]), p, preferred_element_type=_F32)

    handover = [("next", n) for n in range(QK_AHEAD)]

    m_cur = [None] * N_LANES
    own_items = [("own", ln, qs) for ln, (qs, _) in enumerate(lanes)]
    own_items += [("past", ln, jb) for jb in range(Q_SUB) for ln, (qs, _) in enumerate(lanes) if jb < qs]

    def own_score(item):
        kind, ln = item[:2]
        return qk(k_ref[0, 0], ln) if kind == "next" else qk(k_own[item[2]], ln)

    def own_softmax(item, s):
        kind, ln = item[:2]
        if kind == "next":
            s_ref[ln] = s
            return None
        if kind == "past":
            return past_softmax(ln, i0 + item[2], s, m_cur)
        s = jnp.where(key_pos <= qry_pos, s, neg_inf)
        m_cur[ln] = jnp.max(s, axis=0, keepdims=True)
        return jnp.exp2(s - m_cur[ln]).astype(_BF16), None

    def own_accumulate(item, p_alpha):
        kind, ln = item[:2]
        if kind == "own":
            acc_ref[ln] = jnp.dot(v_ext(v_own[item[2]], lanes[ln][1]), p_alpha[0], preferred_element_type=_F32)
        elif kind == "past":
            accumulate(ln, v_own[item[2]], *p_alpha)

    software_pipeline(own_items + handover, own_score, own_softmax, own_accumulate)

    def tile_of(n):
        return divmod(n, N_LANES)

    def past_blocks(n_blocks, first, t, m_run):
        j0 = first + t * n_blocks
        j_next = jnp.minimum(j0 + n_blocks, nb - 1)
        k_j = [k_ref[0, j0 + u] for u in range(n_blocks)]
        v_j = [vt_ref[0, j0 + u] for u in range(n_blocks)]
        m_trip = list(m_run)

        def score(item):
            kind, n = item
            u, ln = tile_of(n)
            if kind == "next":
                return qk(k_ref[0, j_next], n)
            return s_ref[n] if n < QK_AHEAD else qk(k_j[u], ln)

        def softmax(item, s):
            kind, n = item
            if kind == "next":
                s_ref[n] = s
                return None
            u, ln = tile_of(n)
            return past_softmax(ln, j0 + u, s, m_trip)

        def finish(item, p_alpha):
            kind, n = item
            if kind == "tile":
                u, ln = tile_of(n)
                accumulate(ln, v_j[u], *p_alpha)

        software_pipeline([("tile", n) for n in range(n_blocks * N_LANES)] + handover, score, softmax, finish)
        return tuple(m_trip)

    m_run, done = tuple(m_cur), 0
    for n_blocks in KV_TRIPS:
        trips = (i0 - done) // n_blocks
        m_run = lax.fori_loop(0, trips, functools.partial(past_blocks, n_blocks, done), m_run)
        done = done + trips * n_blocks

    for qs in range(Q_SUB):
        outs = []
        for ln in range(qs * N_HEADS, (qs + 1) * N_HEADS):
            acc = acc_ref[ln]
            outs.append(acc[:HEAD_DIM] / acc[HEAD_DIM:HEAD_DIM + 1])
        o_ref[0, qs * blk:(qs + 1) * blk, :] = jnp.concatenate(outs, axis=0).T.astype(o_ref.dtype)


def _moba_attention(q_t, k, v_t, kmean):
    b, nb = k.shape[0], k.shape[1]
    s = nb * MOBA_BLOCK
    assert nb % Q_SUB == 0 and Q_SUB % KV_TRIPS[-1] == 0 and QK_AHEAD <= N_HEADS
    return pl.pallas_call(
        _attn_kernel,
        grid=(b, nb // Q_SUB),
        in_specs=[pl.BlockSpec((1, Q_SUB, ATTN_WIDTH, MOBA_BLOCK), lambda bi, qi: (bi, qi, 0, 0)),
                  pl.BlockSpec((1, nb, MOBA_BLOCK, ATTN_WIDTH), lambda bi, qi: (bi, 0, 0, 0)),
                  pl.BlockSpec((1, nb, ATTN_WIDTH, MOBA_BLOCK), lambda bi, qi: (bi, 0, 0, 0)),
                  pl.BlockSpec((1, nb, ATTN_WIDTH), lambda bi, qi: (bi, 0, 0))],
        out_specs=pl.BlockSpec((1, Q_SUB * MOBA_BLOCK, ATTN_WIDTH), lambda bi, qi: (bi, qi, 0)),
        out_shape=jax.ShapeDtypeStruct((b, s, ATTN_WIDTH), _BF16),
        scratch_shapes=[pltpu.VMEM((N_LANES, 2 * HEAD_DIM, MOBA_BLOCK), _BF16),
                        pltpu.VMEM((N_LANES, nb, MOBA_BLOCK), _F32),
                        pltpu.VMEM((N_LANES, V_EXT, MOBA_BLOCK), _F32),
                        pltpu.VMEM((QK_AHEAD, MOBA_BLOCK, MOBA_BLOCK), _F32)],
        compiler_params=pltpu.CompilerParams(dimension_semantics=("arbitrary", "arbitrary"),
                                             vmem_limit_bytes=VMEM_LIMIT_BYTES),
        name="moba_attention",
    )(q_t, k, v_t, kmean)


_HEAD_W = _Q_COL
_TAIL_W = IN_WIDTH - _ZA_COL
SUB_ROWS = 256


def _out_kernel(x_ref, h_ref, mod_ref, gf_ref, wh_ref, wt_ref, wg_ref, ps_ref, wpu_ref, wau_ref, wo_ref, att_ref,
                out_ref, hist_ref, carry_ref, lv_ref):
    t = pl.program_id(1)
    tm = x_ref.shape[1]
    subs = range(tm // SUB_ROWS)
    rows = [slice(s * SUB_ROWS, (s + 1) * SUB_ROWS) for s in subs]

    def mm(a, w):
        return jnp.dot(a.astype(_BF16), w, preferred_element_type=_F32)

    @pl.when(t == 0)
    def _():
        carry_ref[...] = jnp.zeros_like(carry_ref)

    h = [h_ref[0, r, :] for r in rows]
    u = [mm(h[s], wh_ref[:, :POOL_WIDTH]) for s in subs]
    z_attn = [mm(h[s], wt_ref[:, :ATTN_WIDTH]) for s in subs]
    z_pool = [mm(h[s], wh_ref[:, POOL_WIDTH:]) for s in subs]
    m_attn = [mm(h[s], wt_ref[:, ATTN_WIDTH + D_MODEL:]) for s in subs]
    m_pool = [mm(h[s], wt_ref[:, ATTN_WIDTH:ATTN_WIDTH + D_MODEL]) for s in subs]

    hist_ref[0:POOL_HIST, :] = carry_ref[...]
    for s in subs:
        hist_ref[POOL_HIST + s * SUB_ROWS:POOL_HIST + (s + 1) * SUB_ROWS, :] = u[s]
    carry_ref[...] = u[-1][SUB_ROWS - POOL_HIST:, :]

    y_attn = [mm(att_ref[0, rows[s], :].astype(_F32) * jax.nn.silu(z_attn[s]), wau_ref[...]) for s in subs]

    end = POOL_HIST + tm

    def level(k, r0, r1, lanes):
        return hist_ref[r0:r1, lanes] if k == 0 else lv_ref[k - 1, r0:r1, lanes]

    for k in range(1, len(POOL_WINDOWS)):
        lanes = slice(k * POOL_GC, POOL_WIDTH)
        lo, back = SUBLANES * k, 2 ** (k - 1)
        lv_ref[k - 1, lo:end, lanes] = level(k - 1, lo, end, lanes) + level(k - 1, lo - back, end - back, lanes)

    y_grp = []
    for s in subs:
        r0 = POOL_HIST + s * SUB_ROWS
        pos = t * tm + s * SUB_ROWS + lax.broadcasted_iota(jnp.int32, (SUB_ROWS, POOL_GC), 0)
        ys = []
        for g, w in enumerate(POOL_WINDOWS):
            lanes = slice(g * POOL_GC, (g + 1) * POOL_GC)
            acc = level(g, r0, r0 + SUB_ROWS, lanes) + level(g, r0 - w // 2, r0 - w // 2 + SUB_ROWS, lanes)
            cnt = jnp.minimum(pos + 1, w).astype(_F32)
            ys.append(mm(acc / cnt - u[s][:, lanes], wg_ref[g]))
        y_grp.append(jnp.concatenate(ys, axis=1))
    y_pool = [mm(y_grp[s] * ps_ref[...] * jax.nn.silu(z_pool[s]), wpu_ref[...]) for s in subs]

    merged = [jax.nn.sigmoid(m_pool[s]) * y_pool[s] + jax.nn.sigmoid(m_attn[s]) * y_attn[s] for s in subs]
    half = SUB_ROWS // 2
    pieces = [(s, 0, SUB_ROWS) for s in subs[:-1]] + [(subs[-1], 0, half), (subs[-1], half, SUB_ROWS)]
    res = [mm(merged[s][r0:r1], wo_ref[...]) for s, r0, r1 in pieces]
    for (s, r0, r1), r in zip(pieces, res):
        piece_rows = slice(s * SUB_ROWS + r0, s * SUB_ROWS + r1)
        xn = x_ref[0, piece_rows, :] + mod_ref[0, 2:3, :] * r
        ms = jnp.mean(xn * xn, axis=-1, keepdims=True)
        out_ref[0, piece_rows, :] = xn * lax.rsqrt(ms + EPS) * gf_ref[...]


def _pool_merge_out(x, h, mod, g_final, w_all, w_grp, pool_scale, w_pool_up, w_attn_up, w_out, att):
    assert _ZA_COL % _TAIL_W == 0
    b, s, d = x.shape
    tm = TM_OUT
    full = lambda *shape: pl.BlockSpec(shape, lambda bi, ti: (0,) * len(shape))
    return pl.pallas_call(
        _out_kernel,
        grid=(b, s // tm),
        in_specs=[pl.BlockSpec((1, tm, d), lambda bi, ti: (bi, ti, 0)),
                  pl.BlockSpec((1, tm, d), lambda bi, ti: (bi, ti, 0)),
                  pl.BlockSpec((1, 3, d), lambda bi, ti: (bi, 0, 0)),
                  full(1, d),
                  pl.BlockSpec((d, _HEAD_W), lambda bi, ti: (0, 0)),
                  pl.BlockSpec((d, _TAIL_W), lambda bi, ti: (0, _ZA_COL // _TAIL_W)),
                  full(len(POOL_WINDOWS), POOL_GC, POOL_GC),
                  full(1, POOL_WIDTH),
                  full(POOL_WIDTH, d), full(ATTN_WIDTH, d), full(d, d),
                  pl.BlockSpec((1, tm, ATTN_WIDTH), lambda bi, ti: (bi, ti, 0))],
        out_specs=pl.BlockSpec((1, tm, d), lambda bi, ti: (bi, ti, 0)),
        out_shape=jax.ShapeDtypeStruct((b, s, d), _F32),
        scratch_shapes=[pltpu.VMEM((POOL_HIST + tm, POOL_WIDTH), _F32),
                        pltpu.VMEM((POOL_HIST, POOL_WIDTH), _F32),
                        pltpu.VMEM((len(POOL_WINDOWS) - 1, POOL_HIST + tm, POOL_WIDTH), _F32)],
        compiler_params=pltpu.CompilerParams(dimension_semantics=("arbitrary", "arbitrary"),
                                             vmem_limit_bytes=VMEM_LIMIT_BYTES),
        name="pool_merge_out",
    )(x, h, mod, g_final, w_all, w_all, w_grp, pool_scale, w_pool_up, w_attn_up, w_out, att)


def _rope_tables(s):
    pos = jnp.arange(s, dtype=_F32)
    inv_freq = ROPE_THETA ** (-jnp.arange(0, ROT_DIM, 2, dtype=_F32) / ROT_DIM)
    ang_t = inv_freq[:, None] * pos[None, :]
    dim = jnp.arange(LANES) % HEAD_DIM
    rot = dim < ROT_DIM
    ang_l = pos[:, None] * inv_freq[dim % ROT_HALF][None, :]
    cos_l = jnp.where(rot, jnp.cos(ang_l), 1.0)
    sin_l = jnp.where(rot, jnp.where(dim < ROT_HALF, -1.0, 1.0) * jnp.sin(ang_l), 0.0)
    return jnp.cos(ang_t), jnp.sin(ang_t), cos_l, sin_l


def kernel(x, c, w_ada, b_ada, g_norm, w_in, w_pool_grp, pool_scale, w_pool_up, w_attn_up, w_out, g_final):
    b, s, d = x.shape
    assert w_ada.shape[0] == 1, "single-layer configuration only"
    assert d == D_MODEL and s % TM_QKV == 0 and s % TM_OUT == 0
    assert w_in.shape[1:] == (d, IN_WIDTH)
    w_all = w_in[0].astype(_BF16)
    wq_t = w_all[:, _Q_COL:_K_COL].T
    wv_t = w_all[:, _V_COL:_ZA_COL].T

    mod = _adaln_mod(c, w_ada[0], b_ada[0])
    cos_t, sin_t, cos_l, sin_l = _rope_tables(s)
    g1 = g_norm[0].reshape(1, d)
    q_t, k, v_t, kmean, h = _qkv_proj(x, mod, g1, wq_t, wv_t, w_all, cos_t, sin_t, cos_l, sin_l)
    att = _moba_attention(q_t, k, v_t, kmean.reshape(b, s // MOBA_BLOCK, ATTN_WIDTH))
    return _pool_merge_out(x, h, mod, g_final.reshape(1, d), w_all,
                           w_pool_grp[0].astype(_BF16), pool_scale[0].reshape(1, POOL_WIDTH),
                           w_pool_up[0].astype(_BF16), w_attn_up[0].astype(_BF16), w_out[0].astype(_BF16), att)
```
